```python
import jax
import jax.numpy as jnp
from jax import lax
import numpy as np

D_MODEL = 1024
BATCH = 8
SEQ = 4096
DEPTH = 4

D_A = D_MODEL // 2
A_HEAD_DIM = 128
A_HEADS = D_A // A_HEAD_DIM
A_CHUNK = 64
D_B = D_MODEL // 4
B_BLOCKS = 4
B_BLOCK_DIM = D_B // B_BLOCKS
CONV_WIDTH = 4
LRU_C = 8.0
D_C = D_MODEL // 4
C_GROUPS = 4
C_GROUP_DIM = D_C // C_GROUPS
C_CHUNK = 128

D_MIX = D_A + D_B + D_C
SPLIT_SIZES = (D_A, D_A, D_A, D_A, D_B, D_B, D_C, D_C)
D_IN = sum(SPLIT_SIZES)
D_FF = ((8 * D_MODEL // 3 + 127) // 128) * 128
EPS = 1e-6

kernel_name = 'hybrid_hgrn2_rglru_sgu_macaron'


def rmsnorm(x, gain):
    x32 = x.astype(jnp.float32)
    y = x32 * lax.rsqrt(jnp.mean(x32 * x32, axis=-1, keepdims=True) + EPS)
    return (y * gain.astype(jnp.float32)).astype(x.dtype)


def group_rmsnorm(x, gain, n_groups):
    shp = x.shape
    x32 = x.astype(jnp.float32).reshape(shp[:-1] + (n_groups, shp[-1] // n_groups))
    y = x32 * lax.rsqrt(jnp.mean(x32 * x32, axis=-1, keepdims=True) + EPS)
    return y.reshape(shp) * gain.astype(jnp.float32)


def swiglu(x, w_gate, w_up, w_down):
    return (jax.nn.silu(x @ w_gate) * (x @ w_up)) @ w_down


def hgrn2(q, f_logit, i, g, lower_bound, norm_gain):
    bsz, seq, _ = q.shape
    n_chunks = seq // A_CHUNK
    f32 = jnp.float32
    q = jax.nn.silu(q.astype(f32))
    lb = lower_bound.astype(f32)
    forget = lb + (1.0 - lb) * jax.nn.sigmoid(f_logit.astype(f32))
    k = 1.0 - forget
    log_f = jnp.log(forget)

    def to_chunks(t):
        return t.reshape(bsz, n_chunks, A_CHUNK, A_HEADS, A_HEAD_DIM).transpose(1, 0, 3, 2, 4)

    qc, kc, vc = to_chunks(q), to_chunks(k), to_chunks(i.astype(f32))
    bc = jnp.cumsum(to_chunks(log_f), axis=3)
    causal = jnp.tril(jnp.ones((A_CHUNK, A_CHUNK), bool))[:, :, None]

    def chunk_step(state, inp):
        q_t, k_t, v_t, b_t = inp
        diff = jnp.where(causal, b_t[:, :, :, None, :] - b_t[:, :, None, :, :], -jnp.inf)
        scores = jnp.einsum('bhtk,bhtsk,bhsk->bhts', q_t, jnp.exp(diff), k_t)
        out = (jnp.einsum('bhts,bhsv->bhtv', scores, v_t)
               + jnp.einsum('bhtk,bhkv->bhtv', q_t * jnp.exp(b_t), state))
        b_end = b_t[:, :, -1:, :]
        state = (jnp.exp(b_end[:, :, 0, :, None]) * state
                 + jnp.einsum('bhsk,bhsv->bhkv', k_t * jnp.exp(b_end - b_t), v_t))
        return state, out

    state0 = jnp.zeros((bsz, A_HEADS, A_HEAD_DIM, A_HEAD_DIM), f32)
    _, o = lax.scan(chunk_step, state0, (qc, kc, vc, bc))
    o = o.transpose(1, 0, 3, 2, 4).reshape(bsz, seq, D_A)
    o = group_rmsnorm(o, norm_gain, A_HEADS) * jax.nn.silu(g.astype(f32))
    return o.astype(g.dtype)


def rglru(xb, gate, conv_w, conv_b, w_a, b_a, w_x, b_x, lam, norm_gain):
    bsz, seq, _ = xb.shape
    f32 = jnp.float32
    xp = jnp.pad(xb, ((0, 0), (CONV_WIDTH - 1, 0), (0, 0)))
    xc = conv_b + xp[:, 0:seq] * conv_w[0]
    for tap in range(1, CONV_WIDTH):
        xc = xc + xp[:, tap:tap + seq] * conv_w[tap]
    xh = xc.reshape(bsz, seq, B_BLOCKS, B_BLOCK_DIM)
    r = jax.nn.sigmoid((jnp.einsum('blhi,hij->blhj', xh, w_a) + b_a).astype(f32)).reshape(bsz, seq, D_B)
    gate_in = jax.nn.sigmoid((jnp.einsum('blhi,hij->blhj', xh, w_x) + b_x).astype(f32)).reshape(bsz, seq, D_B)
    log_a = -LRU_C * r * jax.nn.softplus(-lam.astype(f32))
    a = jnp.exp(log_a)
    mult = jnp.sqrt(-jnp.expm1(2.0 * log_a))
    u = mult * gate_in * xc.astype(f32)

    def combine(left, right):
        return right[0] * left[0], right[0] * left[1] + right[1]

    _, h = lax.associative_scan(combine, (a, u), axis=1)
    y = h * jax.nn.gelu(gate.astype(f32))
    return group_rmsnorm(y, norm_gain, B_BLOCKS).astype(xb.dtype)


def chunked_sgu(u_in, v_in, w_s, b_s, norm_gain):
    bsz, seq, _ = u_in.shape
    n_chunks = seq // C_CHUNK
    f32 = jnp.float32
    u = jax.nn.gelu(u_in.astype(f32))
    v = jax.nn.gelu(v_in.astype(f32)).reshape(bsz, n_chunks, C_CHUNK, C_GROUPS, C_GROUP_DIM)
    mu = jnp.mean(v, axis=-1, keepdims=True)
    var = jnp.mean(jnp.square(v - mu), axis=-1, keepdims=True)
    v = (v - mu) * lax.rsqrt(var + EPS)
    w = w_s.astype(f32) * jnp.tril(jnp.ones((C_CHUNK, C_CHUNK), f32))
    z = jnp.einsum('gts,bnsgc->bntgc', w, v) + b_s.astype(f32).T[:, :, None]
    y = u * z.reshape(bsz, seq, D_C)
    return group_rmsnorm(y, norm_gain, C_GROUPS).astype(u_in.dtype)


def setup_inputs(seed: int = 0) -> dict:
    key = jax.random.key(seed)
    ks = jax.random.split(key, 32)
    f32 = jnp.float32

    def nrm(k, shape, scale):
        return jax.random.normal(k, shape, f32) * scale

    def gain(k, shape):
        return 1.0 + 0.05 * jax.random.normal(k, shape, f32)

    a0 = jax.random.uniform(ks[15], (DEPTH, D_B), f32, 0.9, 0.999)
    lam = jnp.log(a0) - jnp.log1p(-a0)
    return {
        'x': jax.random.normal(ks[0], (BATCH, SEQ, D_MODEL), f32),
        'ffn1_norm': gain(ks[1], (DEPTH, D_MODEL)),
        'ffn1_wg': nrm(ks[2], (DEPTH, D_MODEL, D_FF), D_MODEL ** -0.5),
        'ffn1_wu': nrm(ks[3], (DEPTH, D_MODEL, D_FF), D_MODEL ** -0.5),
        'ffn1_wd': nrm(ks[4], (DEPTH, D_FF, D_MODEL), D_FF ** -0.5),
        'mix_norm': gain(ks[5], (DEPTH, D_MODEL)),
        'w_in': nrm(ks[6], (DEPTH, D_MODEL, D_IN), D_MODEL ** -0.5),
        'hgrn_lb_logits': nrm(ks[7], (DEPTH, D_A), 0.5),
        'hgrn_norm': gain(ks[8], (DEPTH, D_A)),
        'conv_w': nrm(ks[9], (DEPTH, CONV_WIDTH, D_B), CONV_WIDTH ** -0.5),
        'conv_b': nrm(ks[10], (DEPTH, D_B), 0.02),
        'lru_wa': nrm(ks[11], (DEPTH, B_BLOCKS, B_BLOCK_DIM, B_BLOCK_DIM), B_BLOCK_DIM ** -0.5),
        'lru_ba': nrm(ks[12], (DEPTH, B_BLOCKS, B_BLOCK_DIM), 0.02),
        'lru_wx': nrm(ks[13], (DEPTH, B_BLOCKS, B_BLOCK_DIM, B_BLOCK_DIM), B_BLOCK_DIM ** -0.5),
        'lru_bx': nrm(ks[14], (DEPTH, B_BLOCKS, B_BLOCK_DIM), 0.02),
        'lru_lambda': lam,
        'lru_norm': gain(ks[16], (DEPTH, D_B)),
        'sgu_w': nrm(ks[17], (DEPTH, C_GROUPS, C_CHUNK, C_CHUNK), C_CHUNK ** -0.5),
        'sgu_b': gain(ks[18], (DEPTH, C_GROUPS, C_CHUNK)),
        'sgu_norm': gain(ks[19], (DEPTH, D_C)),
        'w_out': nrm(ks[20], (DEPTH, D_MIX, D_MODEL), D_MIX ** -0.5),
        'ffn2_norm': gain(ks[21], (DEPTH, D_MODEL)),
        'ffn2_wg': nrm(ks[22], (DEPTH, D_MODEL, D_FF), D_MODEL ** -0.5),
        'ffn2_wu': nrm(ks[23], (DEPTH, D_MODEL, D_FF), D_MODEL ** -0.5),
        'ffn2_wd': nrm(ks[24], (DEPTH, D_FF, D_MODEL), D_FF ** -0.5),
        'final_norm': gain(ks[25], (D_MODEL,)),
    }


def reference(x, ffn1_norm, ffn1_wg, ffn1_wu, ffn1_wd, mix_norm, w_in, hgrn_lb_logits, hgrn_norm,
              conv_w, conv_b, lru_wa, lru_ba, lru_wx, lru_bx, lru_lambda, lru_norm,
              sgu_w, sgu_b, sgu_norm, w_out, ffn2_norm, ffn2_wg, ffn2_wu, ffn2_wd, final_norm):
    lb_soft = jax.nn.softmax(hgrn_lb_logits.astype(jnp.float32), axis=0)
    lower_bounds = jnp.cumsum(lb_soft, axis=0) - lb_soft[0]
    split_points = [int(p) for p in np.cumsum(SPLIT_SIZES)[:-1]]
    h = x
    for layer in range(DEPTH):
        h = h + 0.5 * swiglu(rmsnorm(h, ffn1_norm[layer]), ffn1_wg[layer], ffn1_wu[layer], ffn1_wd[layer])
        z = rmsnorm(h, mix_norm[layer]) @ w_in[layer]
        q, f_logit, i, g, xb, gate, u, v = jnp.split(z, split_points, axis=-1)
        out_a = hgrn2(q, f_logit, i, g, lower_bounds[layer], hgrn_norm[layer])
        out_b = rglru(xb, gate, conv_w[layer], conv_b[layer], lru_wa[layer], lru_ba[layer],
                      lru_wx[layer], lru_bx[layer], lru_lambda[layer], lru_norm[layer])
        out_c = chunked_sgu(u, v, sgu_w[layer], sgu_b[layer], sgu_norm[layer])
        h = h + jnp.concatenate([out_a, out_b, out_c], axis=-1) @ w_out[layer]
        h = h + 0.5 * swiglu(rmsnorm(h, ffn2_norm[layer]), ffn2_wg[layer], ffn2_wu[layer], ffn2_wd[layer])
    return rmsnorm(h, final_norm)
```

```python
import functools

import numpy as np
import jax
import jax.numpy as jnp
from jax import lax
from jax.experimental import pallas as pl
from jax.experimental.pallas import tpu as pltpu

F32 = jnp.float32
BF16 = jnp.bfloat16

EPS = 1e-6
LRU_C = 8.0
CONV_WIDTH = 4

HGRN_HEAD_DIM = 128
LRU_BLOCKS = 4
SGU_GROUPS = 4
SGU_CHUNK = 128

FFN_ROWS = 256
MIX_ROWS = 256
HGRN_CHUNK = 64
HGRN_SUB = 16
EXP_CLAMP = 80.0
VMEM_LIMIT_BYTES = 56 * 1024 * 1024


def _sigmoid(x):
    return 1.0 / (1.0 + jnp.exp(-x))


def _gelu_tanh(x):
    c = np.float32(np.sqrt(2.0 / np.pi))
    return 0.5 * x * (1.0 + jnp.tanh(c * (x + 0.044715 * (x * x * x))))


def _rmsnorm(x, gain):
    ms = jnp.mean(x * x, axis=-1, keepdims=True)
    return x * lax.rsqrt(ms + EPS) * gain


def _split3(x):
    hi = x.astype(BF16)
    r1 = x - hi.astype(F32)
    mid = r1.astype(BF16)
    lo = (r1 - mid.astype(F32)).astype(BF16)
    return hi, mid, lo


def _split2(x):
    hi = x.astype(BF16)
    lo = (x - hi.astype(F32)).astype(BF16)
    return hi, lo


def _group_mean(x, gmat):
    hi, lo = _split2(x)
    return (jnp.dot(hi, gmat, preferred_element_type=F32)
            + jnp.dot(lo, gmat, preferred_element_type=F32))


def _ffn_kernel(x_ref, gain_ref, wg_ref, wu_ref, wd_ref, fgain_ref, o_ref, *, final_norm):
    x = x_ref[...]
    xn = _rmsnorm(x, gain_ref[...]).astype(BF16)
    g = jnp.dot(xn, wg_ref[...], preferred_element_type=F32)
    u = jnp.dot(xn, wu_ref[...], preferred_element_type=F32)
    a = (g * _sigmoid(g) * u).astype(BF16)
    y = x + 0.5 * jnp.dot(a, wd_ref[...], preferred_element_type=F32)
    if final_norm:
        y = _rmsnorm(y, fgain_ref[...])
    o_ref[...] = y


def _ffn_call(h2d, layer, gain, wg, wu, wd, fgain, final_norm):
    n_rows, d_model = h2d.shape
    d_ff = wg.shape[-1]
    resident = pl.Buffered(1)
    return pl.pallas_call(
        functools.partial(_ffn_kernel, final_norm=final_norm),
        grid=(n_rows // FFN_ROWS,),
        in_specs=[
            pl.BlockSpec((FFN_ROWS, d_model), lambda i: (i, 0)),
            pl.BlockSpec((None, 1, d_model), lambda i: (layer, 0, 0)),
            pl.BlockSpec((None, d_model, d_ff), lambda i: (layer, 0, 0), pipeline_mode=resident),
            pl.BlockSpec((None, d_model, d_ff), lambda i: (layer, 0, 0), pipeline_mode=resident),
            pl.BlockSpec((None, d_ff, d_model), lambda i: (layer, 0, 0), pipeline_mode=resident),
            pl.BlockSpec((1, d_model), lambda i: (0, 0)),
        ],
        out_specs=pl.BlockSpec((FFN_ROWS, d_model), lambda i: (i, 0)),
        out_shape=jax.ShapeDtypeStruct(h2d.shape, F32),
        compiler_params=pltpu.CompilerParams(
            dimension_semantics=("arbitrary",), vmem_limit_bytes=VMEM_LIMIT_BYTES),
        name="ffn",
    )(h2d, gain, wg, wu, wd, fgain)


def _hgrn_constants():
    c, n = HGRN_CHUNK, HGRN_SUB
    nb = c // n
    t = np.arange(c)
    blk = t // n
    same = blk[:, None] == blk[None, :]
    incl = same & (t[None, :] <= t[:, None])
    later = same & (t[None, :] > t[:, None])
    rows = [incl, later]
    before = np.stack([blk < i for i in range(nb)])
    after = np.stack([blk > i for i in range(nb)])
    own = np.stack([blk == i for i in range(nb)])
    n_rows = 2 * c + 16 * ((3 * nb + 15) // 16)
    mat = np.zeros((n_rows, c), np.float32)
    mat[:c] = incl
    mat[c:2 * c] = later
    mat[2 * c:2 * c + nb] = before
    mat[2 * c + nb:2 * c + 2 * nb] = after
    mat[2 * c + 2 * nb:2 * c + 3 * nb] = own
    return mat


def _mixer_kernel(h_ref, gain_ref, win_ref, lbl_ref, hnorm_ref, convw_ref, convb_ref,
                  wax_ref, bax_ref, lam_ref, lnorm_ref, sguw_ref, sgub_ref, snorm_ref,
                  wout_ref, csum_ref, g64_ref,
                  o_ref,
                  z_ref, mix_ref, state_ref, hlru_ref, xtail_ref, *, layer):
    rows = h_ref.shape[0]
    d_a = hnorm_ref.shape[-1]
    d_b = lnorm_ref.shape[-1]
    d_c = snorm_ref.shape[-1]
    n_heads = d_a // HGRN_HEAD_DIM
    hd = HGRN_HEAD_DIM
    c, n = HGRN_CHUNK, HGRN_SUB
    nb = c // n

    @pl.when(pl.program_id(1) == 0)
    def _reset_state():
        state_ref[...] = jnp.zeros_like(state_ref)
        hlru_ref[...] = jnp.zeros_like(hlru_ref)
        xtail_ref[...] = jnp.zeros_like(xtail_ref)

    x = h_ref[...]
    xn = _rmsnorm(x, gain_ref[...]).astype(BF16)
    z_ref[...] = jnp.dot(xn, win_ref[...], preferred_element_type=F32)

    logits = lbl_ref[...]
    lmax = jnp.max(logits, axis=0, keepdims=True)
    lexp = jnp.exp(logits - lmax)
    lsum = jnp.sum(lexp, axis=0, keepdims=True)
    if layer == 0:
        lb = jnp.zeros((1, d_a), F32)
    else:
        lb = jnp.sum(lexp[1:layer + 1], axis=0, keepdims=True) / lsum

    row_c = lax.broadcasted_iota(jnp.int32, (c, c), 0)
    col_c = lax.broadcasted_iota(jnp.int32, (c, c), 1)
    blk_r = row_c // n
    blk_c = col_c // n
    mask_diag = (blk_r == blk_c) & (col_c <= row_c)
    half = nb // 2
    assert nb in (2, 4)
    mask_adj_inner = (blk_r == blk_c + 1) & (blk_r != half)
    rowblk = lax.broadcasted_iota(jnp.int32, (c, d_a), 0) // n
    csum = csum_ref[...]
    hgain = hnorm_ref[...]

    def bcast_blocks(vecs):
        out = jnp.broadcast_to(vecs[0:1], (c, d_a))
        for i in range(1, nb):
            out = jnp.where(rowblk == i, jnp.broadcast_to(vecs[i:i + 1], (c, d_a)), out)
        return out

    def hgrn_chunk(ci, carry):
        r0 = pl.multiple_of(ci * c, c)
        q = z_ref[pl.ds(r0, c), 0:d_a]
        f = z_ref[pl.ds(r0, c), d_a:2 * d_a]
        v = z_ref[pl.ds(r0, c), 2 * d_a:3 * d_a]
        g = z_ref[pl.ds(r0, c), 3 * d_a:4 * d_a]
        qs = q * _sigmoid(q)
        forget = lb + (1.0 - lb) * _sigmoid(f)
        kk = 1.0 - forget
        lf = jnp.log(forget)
        pieces = jnp.concatenate(_split3(lf), axis=1)
        sums3 = jnp.dot(csum, pieces, preferred_element_type=F32)
        sums = sums3[:, 0:d_a] + sums3[:, d_a:2 * d_a] + sums3[:, 2 * d_a:3 * d_a]
        cum_in = sums[0:c]
        cum_out = sums[c:2 * c]
        before = sums[2 * c:2 * c + nb]
        after = sums[2 * c + nb:2 * c + 2 * nb]
        own = sums[2 * c + 2 * nb:2 * c + 3 * nb]
        e_in = jnp.exp(cum_in)
        e_out = jnp.exp(cum_out)
        e_inv = jnp.exp(jnp.minimum(-cum_in, EXP_CLAMP))
        q_loc = qs * e_in
        k_end = kk * e_out
        k_loc = kk * e_inv
        zero_row = jnp.zeros((1, d_a), F32)
        qf_rows, kf_rows = [], []
        for i in range(nb):
            if i < half:
                qf_rows.append(zero_row)
                kf_rows.append(jnp.exp(jnp.sum(own[i + 1:half], axis=0, keepdims=True))
                               if i + 1 < half else jnp.ones((1, d_a), F32))
            else:
                kf_rows.append(zero_row)
                qf_rows.append(jnp.exp(jnp.sum(own[half:i], axis=0, keepdims=True))
                               if i > half else jnp.ones((1, d_a), F32))
        q_far = q_loc * bcast_blocks(jnp.concatenate(qf_rows, axis=0))
        k_far = k_end * bcast_blocks(jnp.concatenate(kf_rows, axis=0))
        q_state = q_loc * bcast_blocks(jnp.exp(before))
        k_state = k_end * bcast_blocks(jnp.exp(after))
        chunk_decay = jnp.exp(before[nb - 1:nb] + own[nb - 1:nb])

        q_loc_b = q_loc.astype(BF16)
        k_end_b = k_end.astype(BF16)
        k_loc_b = k_loc.astype(BF16)
        q_far_b = q_far.astype(BF16)
        k_far_b = k_far.astype(BF16)
        q_state_b = q_state.astype(BF16)
        k_state_b = k_state.astype(BF16)
        v_b = v.astype(BF16)
        nt = (((1,), (1,)), ((), ()))
        tn = (((0,), (0,)), ((), ()))
        outs = []
        for hh in range(n_heads):
            sl = slice(hh * hd, (hh + 1) * hd)
            s_diag = lax.dot_general(q_loc_b[:, sl], k_loc_b[:, sl], nt, preferred_element_type=F32)
            s_adj = lax.dot_general(q_loc_b[:, sl], k_end_b[:, sl], nt, preferred_element_type=F32)
            s_far = lax.dot_general(q_far_b[:, sl], k_far_b[:, sl], nt, preferred_element_type=F32)
            scores = jnp.where(mask_diag, s_diag, jnp.where(mask_adj_inner, s_adj, s_far))
            state_t = state_ref[hh]
            o_h = (jnp.dot(scores.astype(BF16), v_b[:, sl], preferred_element_type=F32)
                   + lax.dot_general(q_state_b[:, sl], state_t.astype(BF16), nt,
                                     preferred_element_type=F32))
            upd_t = lax.dot_general(v_b[:, sl], k_state_b[:, sl], tn, preferred_element_type=F32)
            state_ref[hh] = chunk_decay[:, sl] * state_t + upd_t
            ms = jnp.mean(o_h * o_h, axis=-1, keepdims=True)
            outs.append(o_h * lax.rsqrt(ms + EPS))
        o = jnp.concatenate(outs, axis=1) * hgain * (g * _sigmoid(g))
        mix_ref[pl.ds(r0, c), 0:d_a] = o
        return carry

    lax.fori_loop(0, rows // c, hgrn_chunk, 0)

    zb0 = 4 * d_a
    xb = z_ref[:, zb0:zb0 + d_b]
    gate = z_ref[:, zb0 + d_b:zb0 + 2 * d_b]
    tail = xtail_ref[...]
    row8 = lax.broadcasted_iota(jnp.int32, (8, d_b), 0)
    xc = convb_ref[...] + xb * convw_ref[CONV_WIDTH - 1:CONV_WIDTH, :]
    for k in range(1, CONV_WIDTH):
        rolled = pltpu.roll(xb, k, axis=0)
        head = jnp.where(row8 < k, pltpu.roll(tail, k, axis=0), rolled[0:8])
        shifted = jnp.concatenate([head, rolled[8:]], axis=0)
        xc = xc + shifted * convw_ref[CONV_WIDTH - 1 - k:CONV_WIDTH - k, :]
    xtail_ref[...] = xb[rows - 8:rows]

    gates = jnp.dot(xc.astype(BF16), wax_ref[...], preferred_element_type=F32) + bax_ref[...]
    r = _sigmoid(gates[:, 0:d_b])
    gate_in = _sigmoid(gates[:, d_b:2 * d_b])
    neg_lam = -lam_ref[...]
    softplus = jnp.maximum(neg_lam, 0.0) + jnp.log1p(jnp.exp(-jnp.abs(neg_lam)))
    log_a = (-LRU_C) * r * softplus
    a = jnp.exp(log_a)
    mult = jnp.sqrt(-jnp.tanh(log_a) * (a * a + 1.0))
    u = mult * gate_in * xc

    row_t = lax.broadcasted_iota(jnp.int32, (rows, d_b), 0)
    shift = 1
    while shift < rows:
        keep = row_t >= shift
        u = jnp.where(keep, a * pltpu.roll(u, shift, axis=0) + u, u)
        a = jnp.where(keep, a * pltpu.roll(a, shift, axis=0), a)
        shift *= 2
    hseq = u + a * hlru_ref[...]
    hlru_ref[...] = hseq[rows - 1:rows]
    y = hseq * _gelu_tanh(gate)
    g64 = g64_ref[...]
    out_b = y * lax.rsqrt(_group_mean(y * y, g64) + EPS) * lnorm_ref[...]
    mix_ref[:, d_a:d_a + d_b] = out_b

    zc0 = zb0 + 2 * d_b
    tri = (lax.broadcasted_iota(jnp.int32, (SGU_CHUNK, SGU_CHUNK), 1)
           <= lax.broadcasted_iota(jnp.int32, (SGU_CHUNK, SGU_CHUNK), 0))
    gdim = d_c // SGU_GROUPS
    w_causal = [jnp.where(tri, sguw_ref[gi], 0.0).astype(BF16) for gi in range(SGU_GROUPS)]
    for ci in range(rows // SGU_CHUNK):
        rs = slice(ci * SGU_CHUNK, (ci + 1) * SGU_CHUNK)
        su = _gelu_tanh(z_ref[rs, zc0:zc0 + d_c])
        sv = _gelu_tanh(z_ref[rs, zc0 + d_c:zc0 + 2 * d_c])
        mu = _group_mean(sv, g64)
        dv = sv - mu
        var = _group_mean(dv * dv, g64)
        vn = (dv * lax.rsqrt(var + EPS)).astype(BF16)
        zs = [jnp.dot(w_causal[gi], vn[:, gi * gdim:(gi + 1) * gdim], preferred_element_type=F32)
              for gi in range(SGU_GROUPS)]
        zz = jnp.concatenate(zs, axis=1) + sgub_ref[...]
        ys = su * zz
        out_c = ys * lax.rsqrt(_group_mean(ys * ys, g64) + EPS) * snorm_ref[...]
        mix_ref[rs, d_a + d_b:d_a + d_b + d_c] = out_c

    o_ref[...] = x + jnp.dot(mix_ref[...].astype(BF16), wout_ref[...], preferred_element_type=F32)


def _mixer_call(h, layer, p):
    bsz, seq, d_model = h.shape
    d_in = p["w_in"].shape[-1]
    d_a = p["hgrn_norm"].shape[-1]
    d_b = p["lru_norm"].shape[-1]
    d_c = p["sgu_norm"].shape[-1]
    d_mix = d_a + d_b + d_c
    depth = p["lb_logits"].shape[0]
    n_heads = d_a // HGRN_HEAD_DIM

    def per_layer(shape):
        nd = len(shape)
        return pl.BlockSpec((None,) + shape, lambda b, t: (layer,) + (0,) * nd)

    def whole(shape):
        nd = len(shape)
        return pl.BlockSpec(shape, lambda b, t: (0,) * nd)

    csum = p["csum"]
    in_specs = [
        pl.BlockSpec((None, MIX_ROWS, d_model), lambda b, t: (b, t, 0)),
        per_layer((1, d_model)),
        per_layer((d_model, d_in)),
        whole((depth, d_a)),
        per_layer((1, d_a)),
        per_layer((CONV_WIDTH, d_b)),
        per_layer((1, d_b)),
        per_layer((d_b, 2 * d_b)),
        per_layer((1, 2 * d_b)),
        per_layer((1, d_b)),
        per_layer((1, d_b)),
        per_layer((SGU_GROUPS, SGU_CHUNK, SGU_CHUNK)),
        per_layer((SGU_CHUNK, d_c)),
        per_layer((1, d_c)),
        per_layer((d_mix, d_model)),
        whole(csum.shape),
        whole((d_b, d_b)),
    ]
    return pl.pallas_call(
        functools.partial(_mixer_kernel, layer=layer),
        grid=(bsz, seq // MIX_ROWS),
        in_specs=in_specs,
        out_specs=pl.BlockSpec((None, MIX_ROWS, d_model), lambda b, t: (b, t, 0)),
        out_shape=jax.ShapeDtypeStruct(h.shape, F32),
        scratch_shapes=[
            pltpu.VMEM((MIX_ROWS, d_in), F32),
            pltpu.VMEM((MIX_ROWS, d_mix), F32),
            pltpu.VMEM((n_heads, HGRN_HEAD_DIM, HGRN_HEAD_DIM), F32),
            pltpu.VMEM((1, d_b), F32),
            pltpu.VMEM((8, d_b), F32),
        ],
        compiler_params=pltpu.CompilerParams(
            dimension_semantics=("arbitrary", "arbitrary"), vmem_limit_bytes=VMEM_LIMIT_BYTES),
        name="mixer",
    )(h, p["mix_norm"], p["w_in"], p["lb_logits"], p["hgrn_norm"], p["conv_w"], p["conv_b"],
      p["wax"], p["bax"], p["lam"], p["lru_norm"], p["sgu_w"], p["sgu_b"], p["sgu_norm"],
      p["w_out"], csum, p["g64"])


def _block_diag(w):
    depth, nblk, d, _ = w.shape
    eye = jnp.eye(nblk, dtype=w.dtype)
    return jnp.einsum("lhij,hg->lhigj", w, eye).reshape(depth, nblk * d, nblk * d)


def kernel(x, ffn1_norm, ffn1_wg, ffn1_wu, ffn1_wd, mix_norm, w_in, hgrn_lb_logits, hgrn_norm,
           conv_w, conv_b, lru_wa, lru_ba, lru_wx, lru_bx, lru_lambda, lru_norm,
           sgu_w, sgu_b, sgu_norm, w_out, ffn2_norm, ffn2_wg, ffn2_wu, ffn2_wd, final_norm):
    bsz, seq, d_model = x.shape
    depth = w_in.shape[0]
    d_b = lru_norm.shape[-1]
    d_c = sgu_norm.shape[-1]
    assert seq % MIX_ROWS == 0 and (bsz * seq) % FFN_ROWS == 0
    assert MIX_ROWS % HGRN_CHUNK == 0 and MIX_ROWS % SGU_CHUNK == 0
    assert hgrn_norm.shape[-1] % HGRN_HEAD_DIM == 0

    row = lambda a: a.reshape(depth, 1, a.shape[-1])
    group_id = np.arange(d_b) // (d_b // LRU_BLOCKS)
    g64 = (group_id[:, None] == group_id[None, :]).astype(np.float32) / (d_b // LRU_BLOCKS)
    assert d_c == d_b and SGU_GROUPS == LRU_BLOCKS
    p = {
        "mix_norm": row(mix_norm),
        "w_in": w_in.astype(BF16),
        "lb_logits": hgrn_lb_logits,
        "hgrn_norm": row(hgrn_norm),
        "conv_w": conv_w,
        "conv_b": row(conv_b),
        "wax": jnp.concatenate([_block_diag(lru_wa), _block_diag(lru_wx)], axis=-1).astype(BF16),
        "bax": jnp.concatenate([lru_ba.reshape(depth, 1, d_b), lru_bx.reshape(depth, 1, d_b)], axis=-1),
        "lam": row(lru_lambda),
        "lru_norm": row(lru_norm),
        "sgu_w": sgu_w,
        "sgu_b": jnp.repeat(jnp.swapaxes(sgu_b, 1, 2), d_c // SGU_GROUPS, axis=2),
        "sgu_norm": row(sgu_norm),
        "w_out": w_out.astype(BF16),
        "csum": jnp.asarray(_hgrn_constants(), BF16),
        "g64": jnp.asarray(g64, BF16),
    }
    ffn1 = (row(ffn1_norm), ffn1_wg.astype(BF16), ffn1_wu.astype(BF16), ffn1_wd.astype(BF16))
    ffn2 = (row(ffn2_norm), ffn2_wg.astype(BF16), ffn2_wu.astype(BF16), ffn2_wd.astype(BF16))
    fgain = final_norm.reshape(1, d_model)

    h = x
    for layer in range(depth):
        h = _ffn_call(h.reshape(bsz * seq, d_model), layer, *ffn1, fgain, False)
        h = _mixer_call(h.reshape(bsz, seq, d_model), layer, p)
        h = _ffn_call(h.reshape(bsz * seq, d_model), layer, *ffn2, fgain,
                      layer == depth - 1)
    return h.reshape(bsz, seq, d_model)
```

```python
import functools

import numpy as np
import jax
import jax.numpy as jnp
from jax import lax
from jax.experimental import pallas as pl
from jax.experimental.pallas import tpu as pltpu

F32 = jnp.float32
BF16 = jnp.bfloat16

EPS = 1e-6
LRU_C = 8.0
CONV_WIDTH = 4

HGRN_HEAD_DIM = 128
LRU_BLOCKS = 4
SGU_GROUPS = 4
SGU_CHUNK = 128

SUBLANES = 8
FFN_ROWS = 256
MIX_ROWS = 256
HGRN_CHUNK = 64
HGRN_SUB = 16
INV_DECAY_MAX = float(np.exp(80.0))
VMEM_LIMIT_BYTES = 56 * 1024 * 1024


def _sigmoid(x):
    return 0.5 * jnp.tanh(0.5 * x) + 0.5


def _gelu_tanh(x):
    c = np.float32(np.sqrt(2.0 / np.pi))
    return 0.5 * x * (1.0 + jnp.tanh(c * (x + 0.044715 * (x * x * x))))


def _rmsnorm(x, gain):
    ms = jnp.mean(x * x, axis=-1, keepdims=True)
    return x * lax.rsqrt(ms + EPS) * gain


def _split2(x):
    hi = x.astype(BF16)
    lo = (x - hi.astype(F32)).astype(BF16)
    return hi, lo


def _group_mean(x, gmat):
    hi, lo = _split2(x)
    return (jnp.dot(hi, gmat, preferred_element_type=F32)
            + jnp.dot(lo, gmat, preferred_element_type=F32))


def _ffn_kernel(x_ref, gain_ref, wg_ref, wu_ref, wd_ref, fgain_ref, o_ref, *, final_norm):
    x = x_ref[...]
    xn = _rmsnorm(x, gain_ref[...]).astype(BF16)
    g = jnp.dot(xn, wg_ref[...], preferred_element_type=F32)
    u = jnp.dot(xn, wu_ref[...], preferred_element_type=F32)
    a = (g * _sigmoid(g) * u).astype(BF16)
    y = x + 0.5 * jnp.dot(a, wd_ref[...], preferred_element_type=F32)
    if final_norm:
        y = _rmsnorm(y, fgain_ref[...])
    o_ref[...] = y


def _ffn_call(h2d, layer, gain, wg, wu, wd, fgain, final_norm):
    n_rows, d_model = h2d.shape
    d_ff = wg.shape[-1]
    resident = pl.Buffered(1)
    return pl.pallas_call(
        functools.partial(_ffn_kernel, final_norm=final_norm),
        grid=(n_rows // FFN_ROWS,),
        in_specs=[
            pl.BlockSpec((FFN_ROWS, d_model), lambda i: (i, 0)),
            pl.BlockSpec((None, 1, d_model), lambda i: (layer, 0, 0)),
            pl.BlockSpec((None, d_model, d_ff), lambda i: (layer, 0, 0), pipeline_mode=resident),
            pl.BlockSpec((None, d_model, d_ff), lambda i: (layer, 0, 0), pipeline_mode=resident),
            pl.BlockSpec((None, d_ff, d_model), lambda i: (layer, 0, 0), pipeline_mode=resident),
            pl.BlockSpec((1, d_model), lambda i: (0, 0)),
        ],
        out_specs=pl.BlockSpec((FFN_ROWS, d_model), lambda i: (i, 0)),
        out_shape=jax.ShapeDtypeStruct(h2d.shape, F32),
        compiler_params=pltpu.CompilerParams(
            dimension_semantics=("arbitrary",), vmem_limit_bytes=VMEM_LIMIT_BYTES),
        name="ffn",
    )(h2d, gain, wg, wu, wd, fgain)


def _mixer_kernel(h_ref, gain_ref, win_ref, lbl_ref, hnorm_ref, convw_ref, convb_ref,
                  wax_ref, bax_ref, lam_ref, lnorm_ref, sguw_ref, sgub_ref, snorm_ref,
                  wout_ref, g64_ref,
                  o_ref,
                  z_ref, mix_ref, state_ref, hlru_ref, xtail_ref, *, layer):
    rows = h_ref.shape[0]
    d_a = hnorm_ref.shape[-1]
    d_b = lnorm_ref.shape[-1]
    d_c = snorm_ref.shape[-1]
    n_heads = d_a // HGRN_HEAD_DIM
    hd = HGRN_HEAD_DIM
    c, n = HGRN_CHUNK, HGRN_SUB
    nb = c // n

    @pl.when(pl.program_id(1) == 0)
    def _reset_state():
        state_ref[...] = jnp.zeros_like(state_ref)
        hlru_ref[...] = jnp.zeros_like(hlru_ref)
        xtail_ref[...] = jnp.zeros_like(xtail_ref)

    x = h_ref[...]
    xn = _rmsnorm(x, gain_ref[...]).astype(BF16)
    z_ref[...] = jnp.dot(xn, win_ref[...], preferred_element_type=F32)

    logits = lbl_ref[...]
    lmax = jnp.max(logits, axis=0, keepdims=True)
    lexp = jnp.exp(logits - lmax)
    lsum = jnp.sum(lexp, axis=0, keepdims=True)
    if layer == 0:
        lb = jnp.zeros((1, d_a), F32)
    else:
        lb = jnp.sum(lexp[1:layer + 1], axis=0, keepdims=True) / lsum
    fg_half = 0.5 * (1.0 - lb)
    fg_mid = lb + fg_half

    row_c = lax.broadcasted_iota(jnp.int32, (c, c), 0)
    col_c = lax.broadcasted_iota(jnp.int32, (c, c), 1)
    dist = row_c // n - col_c // n
    mask_diag = (dist == 0) & (col_c <= row_c)
    row_in_sub = lax.broadcasted_iota(jnp.int32, (c, d_a), 0) % n
    hgain = hnorm_ref[...]
    nt = (((1,), (1,)), ((), ()))
    tn = (((0,), (0,)), ((), ()))

    def sub(a, i):
        return a[i * n:(i + 1) * n]

    for ci in range(rows // c):
        rs = slice(ci * c, (ci + 1) * c)
        q = z_ref[rs, 0:d_a]
        f = z_ref[rs, d_a:2 * d_a]
        v = z_ref[rs, 2 * d_a:3 * d_a]
        g = z_ref[rs, 3 * d_a:4 * d_a]
        qs = q * _sigmoid(q)
        half_th = fg_half * jnp.tanh(0.5 * f)
        forget = fg_mid + half_th
        kk = fg_half - half_th
        e = forget
        shift = 1
        while shift < n:
            e = jnp.where(row_in_sub >= shift, e * pltpu.roll(e, shift, axis=0), e)
            shift *= 2
        e_inv = jnp.minimum(1.0 / e, INV_DECAY_MAX)
        q_loc = qs * e
        k_loc = kk * e_inv
        p = [e[(i + 1) * n - 1:(i + 1) * n] for i in range(nb)]
        before = [None] * nb
        for i in range(1, nb):
            before[i] = p[i - 1] if i == 1 else before[i - 1] * p[i - 1]
        to_end = [None] * nb
        for i in reversed(range(nb)):
            to_end[i] = p[i] if i == nb - 1 else to_end[i + 1] * p[i]
        chunk_decay = to_end[0]
        q_dist = [[sub(q_loc, i) for i in range(nb)]]
        for d in range(1, nb):
            prev = q_dist[d - 1]
            q_dist.append([prev[i - (d - 1)] * p[i - d] for i in range(d, nb)])
        q_stack = jnp.concatenate([blk for grp in q_dist for blk in grp], axis=0).astype(BF16)
        q_state = jnp.concatenate(
            [sub(q_loc, 0)] + [sub(q_loc, i) * before[i] for i in range(1, nb)], axis=0).astype(BF16)
        k_state = jnp.concatenate(
            [sub(k_loc, i) * to_end[i] for i in range(nb)], axis=0).astype(BF16)
        k_loc_b = k_loc.astype(BF16)
        v_b = v.astype(BF16)

        outs = []
        for hh in range(n_heads):
            sl = slice(hh * hd, (hh + 1) * hd)
            s_all = lax.dot_general(q_stack[:, sl], k_loc_b[:, sl], nt, preferred_element_type=F32)
            scores = jnp.where(mask_diag, s_all[0:c], 0.0)
            r0 = c
            for d in range(1, nb):
                grp = jnp.concatenate(
                    [jnp.zeros((d * n, c), F32), s_all[r0:r0 + (nb - d) * n]], axis=0)
                scores = jnp.where(dist == d, grp, scores)
                r0 += (nb - d) * n
            state_t = state_ref[hh]
            o_h = (jnp.dot(scores.astype(BF16), v_b[:, sl], preferred_element_type=F32)
                   + lax.dot_general(q_state[:, sl], state_t.astype(BF16), nt,
                                     preferred_element_type=F32))
            upd_t = lax.dot_general(v_b[:, sl], k_state[:, sl], tn, preferred_element_type=F32)
            state_ref[hh] = chunk_decay[:, sl] * state_t + upd_t
            ms = jnp.mean(o_h * o_h, axis=-1, keepdims=True)
            outs.append(o_h * lax.rsqrt(ms + EPS))
        o = jnp.concatenate(outs, axis=1) * hgain * (g * _sigmoid(g))
        mix_ref[rs, 0:d_a] = o.astype(BF16)

    zb0 = 4 * d_a
    xb = z_ref[:, zb0:zb0 + d_b]
    gate = z_ref[:, zb0 + d_b:zb0 + 2 * d_b]
    tail = xtail_ref[...]
    row8 = lax.broadcasted_iota(jnp.int32, (SUBLANES, d_b), 0)
    xc = convb_ref[...] + xb * convw_ref[CONV_WIDTH - 1:CONV_WIDTH, :]
    for k in range(1, CONV_WIDTH):
        rolled = pltpu.roll(xb, k, axis=0)
        head = jnp.where(row8 < k, pltpu.roll(tail, k, axis=0), rolled[0:SUBLANES])
        shifted = jnp.concatenate([head, rolled[SUBLANES:]], axis=0)
        xc = xc + shifted * convw_ref[CONV_WIDTH - 1 - k:CONV_WIDTH - k, :]
    xtail_ref[...] = xb[rows - SUBLANES:rows]

    gates = jnp.dot(xc.astype(BF16), wax_ref[...], preferred_element_type=F32) + bax_ref[...]
    r = _sigmoid(gates[:, 0:d_b])
    gate_in = _sigmoid(gates[:, d_b:2 * d_b])
    neg_lam = -lam_ref[...]
    softplus = jnp.maximum(neg_lam, 0.0) + jnp.log1p(jnp.exp(-jnp.abs(neg_lam)))
    log_a = (-LRU_C) * r * softplus
    a = jnp.exp(log_a)
    mult = jnp.sqrt(-jnp.tanh(log_a) * (a * a + 1.0))
    u = mult * gate_in * xc

    row_in_grp = lax.broadcasted_iota(jnp.int32, (rows, d_b), 0) % SUBLANES
    shift = 1
    while shift < SUBLANES:
        keep = row_in_grp >= shift
        u = jnp.where(keep, a * pltpu.roll(u, shift, axis=0) + u, u)
        a = jnp.where(keep, a * pltpu.roll(a, shift, axis=0), a)
        shift *= 2
    carry = hlru_ref[...]
    hs = []
    for gi in range(rows // SUBLANES):
        gs = slice(gi * SUBLANES, (gi + 1) * SUBLANES)
        hg = u[gs] + a[gs] * carry
        hs.append(hg)
        carry = hg[SUBLANES - 1:SUBLANES]
    hlru_ref[...] = carry
    hseq = jnp.concatenate(hs, axis=0)
    y = hseq * _gelu_tanh(gate)
    g64 = g64_ref[...]
    out_b = y * lax.rsqrt(_group_mean(y * y, g64) + EPS) * lnorm_ref[...]
    mix_ref[:, d_a:d_a + d_b] = out_b.astype(BF16)

    zc0 = zb0 + 2 * d_b
    tri = (lax.broadcasted_iota(jnp.int32, (SGU_CHUNK, SGU_CHUNK), 1)
           <= lax.broadcasted_iota(jnp.int32, (SGU_CHUNK, SGU_CHUNK), 0))
    gdim = d_c // SGU_GROUPS
    w_causal = [jnp.where(tri, sguw_ref[gi], 0.0).astype(BF16) for gi in range(SGU_GROUPS)]
    for ci in range(rows // SGU_CHUNK):
        rs = slice(ci * SGU_CHUNK, (ci + 1) * SGU_CHUNK)
        su = _gelu_tanh(z_ref[rs, zc0:zc0 + d_c])
        sv = _gelu_tanh(z_ref[rs, zc0 + d_c:zc0 + 2 * d_c])
        mu = _group_mean(sv, g64)
        dv = sv - mu
        var = _group_mean(dv * dv, g64)
        vn = (dv * lax.rsqrt(var + EPS)).astype(BF16)
        zs = [jnp.dot(w_causal[gi], vn[:, gi * gdim:(gi + 1) * gdim], preferred_element_type=F32)
              for gi in range(SGU_GROUPS)]
        zz = jnp.concatenate(zs, axis=1) + sgub_ref[...]
        ys = su * zz
        out_c = ys * lax.rsqrt(_group_mean(ys * ys, g64) + EPS) * snorm_ref[...]
        mix_ref[rs, d_a + d_b:d_a + d_b + d_c] = out_c.astype(BF16)

    o_ref[...] = x + jnp.dot(mix_ref[...], wout_ref[...], preferred_element_type=F32)


def _mixer_call(h, layer, p):
    bsz, seq, d_model = h.shape
    d_in = p["w_in"].shape[-1]
    d_a = p["hgrn_norm"].shape[-1]
    d_b = p["lru_norm"].shape[-1]
    d_c = p["sgu_norm"].shape[-1]
    d_mix = d_a + d_b + d_c
    depth = p["lb_logits"].shape[0]
    n_heads = d_a // HGRN_HEAD_DIM

    def per_layer(shape):
        nd = len(shape)
        return pl.BlockSpec((None,) + shape, lambda b, t: (layer,) + (0,) * nd)

    def whole(shape):
        nd = len(shape)
        return pl.BlockSpec(shape, lambda b, t: (0,) * nd)

    in_specs = [
        pl.BlockSpec((None, MIX_ROWS, d_model), lambda b, t: (b, t, 0)),
        per_layer((1, d_model)),
        per_layer((d_model, d_in)),
        whole((depth, d_a)),
        per_layer((1, d_a)),
        per_layer((CONV_WIDTH, d_b)),
        per_layer((1, d_b)),
        per_layer((d_b, 2 * d_b)),
        per_layer((1, 2 * d_b)),
        per_layer((1, d_b)),
        per_layer((1, d_b)),
        per_layer((SGU_GROUPS, SGU_CHUNK, SGU_CHUNK)),
        per_layer((SGU_CHUNK, d_c)),
        per_layer((1, d_c)),
        per_layer((d_mix, d_model)),
        whole((d_b, d_b)),
    ]
    return pl.pallas_call(
        functools.partial(_mixer_kernel, layer=layer),
        grid=(bsz, seq // MIX_ROWS),
        in_specs=in_specs,
        out_specs=pl.BlockSpec((None, MIX_ROWS, d_model), lambda b, t: (b, t, 0)),
        out_shape=jax.ShapeDtypeStruct(h.shape, F32),
        scratch_shapes=[
            pltpu.VMEM((MIX_ROWS, d_in), F32),
            pltpu.VMEM((MIX_ROWS, d_mix), BF16),
            pltpu.VMEM((n_heads, HGRN_HEAD_DIM, HGRN_HEAD_DIM), F32),
            pltpu.VMEM((1, d_b), F32),
            pltpu.VMEM((SUBLANES, d_b), F32),
        ],
        compiler_params=pltpu.CompilerParams(
            dimension_semantics=("arbitrary", "arbitrary"), vmem_limit_bytes=VMEM_LIMIT_BYTES),
        name="mixer",
    )(h, p["mix_norm"], p["w_in"], p["lb_logits"], p["hgrn_norm"], p["conv_w"], p["conv_b"],
      p["wax"], p["bax"], p["lam"], p["lru_norm"], p["sgu_w"], p["sgu_b"], p["sgu_norm"],
      p["w_out"], p["g64"])


def _block_diag(w):
    depth, nblk, d, _ = w.shape
    eye = jnp.eye(nblk, dtype=w.dtype)
    return jnp.einsum("lhij,hg->lhigj", w, eye).reshape(depth, nblk * d, nblk * d)


def kernel(x, ffn1_norm, ffn1_wg, ffn1_wu, ffn1_wd, mix_norm, w_in, hgrn_lb_logits, hgrn_norm,
           conv_w, conv_b, lru_wa, lru_ba, lru_wx, lru_bx, lru_lambda, lru_norm,
           sgu_w, sgu_b, sgu_norm, w_out, ffn2_norm, ffn2_wg, ffn2_wu, ffn2_wd, final_norm):
    bsz, seq, d_model = x.shape
    depth = w_in.shape[0]
    d_b = lru_norm.shape[-1]
    d_c = sgu_norm.shape[-1]
    assert seq % MIX_ROWS == 0 and (bsz * seq) % FFN_ROWS == 0
    assert MIX_ROWS % HGRN_CHUNK == 0 and MIX_ROWS % SGU_CHUNK == 0
    assert hgrn_norm.shape[-1] % HGRN_HEAD_DIM == 0

    row = lambda a: a.reshape(depth, 1, a.shape[-1])
    group_id = np.arange(d_b) // (d_b // LRU_BLOCKS)
    g64 = (group_id[:, None] == group_id[None, :]).astype(np.float32) / (d_b // LRU_BLOCKS)
    assert d_c == d_b and SGU_GROUPS == LRU_BLOCKS
    p = {
        "mix_norm": row(mix_norm),
        "w_in": w_in.astype(BF16),
        "lb_logits": hgrn_lb_logits,
        "hgrn_norm": row(hgrn_norm),
        "conv_w": conv_w,
        "conv_b": row(conv_b),
        "wax": jnp.concatenate([_block_diag(lru_wa), _block_diag(lru_wx)], axis=-1).astype(BF16),
        "bax": jnp.concatenate([lru_ba.reshape(depth, 1, d_b), lru_bx.reshape(depth, 1, d_b)], axis=-1),
        "lam": row(lru_lambda),
        "lru_norm": row(lru_norm),
        "sgu_w": sgu_w,
        "sgu_b": jnp.repeat(jnp.swapaxes(sgu_b, 1, 2), d_c // SGU_GROUPS, axis=2),
        "sgu_norm": row(sgu_norm),
        "w_out": w_out.astype(BF16),
        "g64": jnp.asarray(g64, BF16),
    }
    ffn1 = (row(ffn1_norm), ffn1_wg.astype(BF16), ffn1_wu.astype(BF16), ffn1_wd.astype(BF16))
    ffn2 = (row(ffn2_norm), ffn2_wg.astype(BF16), ffn2_wu.astype(BF16), ffn2_wd.astype(BF16))
    fgain = final_norm.reshape(1, d_model)

    h = x
    for layer in range(depth):
        h = _ffn_call(h.reshape(bsz * seq, d_model), layer, *ffn1, fgain, False)
        h = _mixer_call(h.reshape(bsz, seq, d_model), layer, p)
        h = _ffn_call(h.reshape(bsz * seq, d_model), layer, *ffn2, fgain,
                      layer == depth - 1)
    return h.reshape(bsz, seq, d_model)
```

```python
import functools

import numpy as np
import jax
import jax.numpy as jnp
from jax import lax
from jax.experimental import pallas as pl
from jax.experimental.pallas import tpu as pltpu

F32 = jnp.float32
BF16 = jnp.bfloat16

EPS = 1e-6
LRU_C = 8.0
CONV_WIDTH = 4

HGRN_HEAD_DIM = 128
LRU_BLOCKS = 4
SGU_GROUPS = 4
SGU_CHUNK = 128

SUBLANES = 8
FFN_ROWS = 512
MIX_ROWS = 256
PROJ_COLS = 256
HGRN_CHUNK = 64
HGRN_SUB = 16
INV_DECAY_MAX = float(np.exp(80.0))
VMEM_LIMIT_BYTES = 56 * 1024 * 1024


def _sigmoid(x):
    return 0.5 * jnp.tanh(0.5 * x) + 0.5


def _silu(x):
    h = 0.5 * x
    return h * jnp.tanh(h) + h


def _gelu_tanh(x):
    c = float(np.sqrt(2.0 / np.pi))
    h = 0.5 * x
    return h * jnp.tanh(x * (c + (c * 0.044715) * (x * x))) + h


def _sqrt_nonneg(w):
    return w * lax.rsqrt(jnp.maximum(w, float(np.finfo(np.float32).tiny)))


def _rmsnorm(x, gain):
    ms = jnp.mean(x * x, axis=-1, keepdims=True)
    return x * lax.rsqrt(ms + EPS) * gain


def _split2(x):
    hi = x.astype(BF16)
    lo = (x - hi.astype(F32)).astype(BF16)
    return hi, lo


def _group_mean(x, gmat):
    hi, lo = _split2(x)
    return (jnp.dot(hi, gmat, preferred_element_type=F32)
            + jnp.dot(lo, gmat, preferred_element_type=F32))


def _ffn_kernel(x_ref, gain_ref, wg_ref, wu_ref, wd_ref, fgain_ref, o_ref, *, final_norm):
    x = x_ref[...]
    xn = _rmsnorm(x, gain_ref[...]).astype(BF16)
    g = jnp.dot(xn, wg_ref[...], preferred_element_type=F32)
    u = jnp.dot(xn, wu_ref[...], preferred_element_type=F32)
    a = (_silu(g) * u).astype(BF16)
    y = x + 0.5 * jnp.dot(a, wd_ref[...], preferred_element_type=F32)
    if final_norm:
        y = _rmsnorm(y, fgain_ref[...])
    o_ref[...] = y


def _ffn_call(h2d, layer, gain, wg, wu, wd, fgain, final_norm):
    n_rows, d_model = h2d.shape
    d_ff = wg.shape[-1]
    resident = pl.Buffered(1)
    return pl.pallas_call(
        functools.partial(_ffn_kernel, final_norm=final_norm),
        grid=(n_rows // FFN_ROWS,),
        in_specs=[
            pl.BlockSpec((FFN_ROWS, d_model), lambda i: (i, 0)),
            pl.BlockSpec((None, 1, d_model), lambda i: (layer, 0, 0)),
            pl.BlockSpec((None, d_model, d_ff), lambda i: (layer, 0, 0), pipeline_mode=resident),
            pl.BlockSpec((None, d_model, d_ff), lambda i: (layer, 0, 0), pipeline_mode=resident),
            pl.BlockSpec((None, d_ff, d_model), lambda i: (layer, 0, 0), pipeline_mode=resident),
            pl.BlockSpec((1, d_model), lambda i: (0, 0)),
        ],
        out_specs=pl.BlockSpec((FFN_ROWS, d_model), lambda i: (i, 0)),
        out_shape=jax.ShapeDtypeStruct(h2d.shape, F32),
        compiler_params=pltpu.CompilerParams(
            dimension_semantics=("arbitrary",), vmem_limit_bytes=VMEM_LIMIT_BYTES),
        name="ffn",
    )(h2d, gain, wg, wu, wd, fgain)


def _project(x_tile, gain_ref, win_ref):
    xn = _rmsnorm(x_tile, gain_ref[...]).astype(BF16)
    return jnp.dot(xn, win_ref[...], preferred_element_type=F32)


def _mixer_kernel(h_ref, hnext_ref, gain_ref, win_ref, *refs, layer, tiles_per_seq):
    z0_ref, z1_ref, state_ref, hlru_ref, xtail_ref = refs[-5:]
    step = pl.program_id(0)

    @pl.when(step % tiles_per_seq == 0)
    def _reset_state():
        state_ref[...] = jnp.zeros_like(state_ref)
        hlru_ref[...] = jnp.zeros_like(hlru_ref)
        xtail_ref[...] = jnp.zeros_like(xtail_ref)

    @pl.when(step == 0)
    def _first_projection():
        z0_ref[...] = _project(h_ref[...], gain_ref, win_ref)

    tile = functools.partial(_mixer_tile, h_ref, hnext_ref, gain_ref, win_ref, *refs[:-5],
                             state_ref, hlru_ref, xtail_ref, layer=layer)
    pl.when(step % 2 == 0)(functools.partial(tile, z0_ref, z1_ref))
    pl.when(step % 2 == 1)(functools.partial(tile, z1_ref, z0_ref))


def _mixer_tile(h_ref, hnext_ref, gain_ref, win_ref, lbl_ref, hnorm_ref, convw_ref, convb_ref,
                wax_ref, bax_ref, lam_ref, lnorm_ref, sguw_ref, sgub_ref, snorm_ref,
                wout_ref, g64_ref, o_ref, state_ref, hlru_ref, xtail_ref,
                zc, zn, *, layer):
    rows = h_ref.shape[0]
    d_a = hnorm_ref.shape[-1]
    d_b = lnorm_ref.shape[-1]
    d_c = snorm_ref.shape[-1]
    d_in = zn.shape[-1]
    n_heads = d_a // HGRN_HEAD_DIM
    hd = HGRN_HEAD_DIM
    c, n = HGRN_CHUNK, HGRN_SUB
    nb = c // n
    zb0 = 4 * d_a
    zc0 = zb0 + 2 * d_b

    g64 = g64_ref[...]
    out = {}

    def w_out_part(stage_out, col0):
        return jnp.dot(stage_out, wout_ref[col0:col0 + stage_out.shape[1], :],
                       preferred_element_type=F32)


    def projection_stream():
        xn_next = _rmsnorm(hnext_ref[...], gain_ref[...]).astype(BF16)
        yield
        for j in range(d_in // PROJ_COLS):
            cs = slice(j * PROJ_COLS, (j + 1) * PROJ_COLS)
            zn[:, cs] = jnp.dot(xn_next, win_ref[:, cs], preferred_element_type=F32)
            yield

    def sgu_stream():
        tri = (lax.broadcasted_iota(jnp.int32, (SGU_CHUNK, SGU_CHUNK), 1)
               <= lax.broadcasted_iota(jnp.int32, (SGU_CHUNK, SGU_CHUNK), 0))
        gdim = d_c // SGU_GROUPS
        w_causal = [jnp.where(tri, sguw_ref[gi], 0.0).astype(BF16) for gi in range(SGU_GROUPS)]
        yield
        outs = []
        for ci in range(rows // SGU_CHUNK):
            rs = slice(ci * SGU_CHUNK, (ci + 1) * SGU_CHUNK)
            su = _gelu_tanh(zc[rs, zc0:zc0 + d_c])
            sv = _gelu_tanh(zc[rs, zc0 + d_c:zc0 + 2 * d_c])
            yield
            mu = _group_mean(sv, g64)
            yield
            dv = sv - mu
            var = _group_mean(dv * dv, g64)
            yield
            vn = (dv * lax.rsqrt(var + EPS)).astype(BF16)
            zs = [jnp.dot(w_causal[gi], vn[:, gi * gdim:(gi + 1) * gdim],
                          preferred_element_type=F32) for gi in range(SGU_GROUPS)]
            yield
            ys = su * (jnp.concatenate(zs, axis=1) + sgub_ref[...])
            ms = _group_mean(ys * ys, g64)
            yield
            outs.append((ys * lax.rsqrt(ms + EPS) * snorm_ref[...]).astype(BF16))
            yield
        out["c"] = w_out_part(jnp.concatenate(outs, axis=0), d_a + d_b)
        yield

    def hgrn_stream():
        logits = lbl_ref[...]
        lexp = jnp.exp(logits - jnp.max(logits, axis=0, keepdims=True))
        if layer == 0:
            lb = jnp.zeros((1, d_a), F32)
        else:
            lb = (jnp.sum(lexp[1:layer + 1], axis=0, keepdims=True)
                  / jnp.sum(lexp, axis=0, keepdims=True))
        fg_half = 0.5 * (1.0 - lb)
        fg_mid = lb + fg_half

        row_c = lax.broadcasted_iota(jnp.int32, (c, c), 0)
        col_c = lax.broadcasted_iota(jnp.int32, (c, c), 1)
        dist = row_c // n - col_c // n
        mask_diag = (dist == 0) & (col_c <= row_c)
        row_in_sub = lax.broadcasted_iota(jnp.int32, (c, d_a), 0) % n
        hgain = hnorm_ref[...]
        nt = (((1,), (1,)), ((), ()))
        tn = (((0,), (0,)), ((), ()))

        def sub(a, i):
            return a[i * n:(i + 1) * n]

        chunks = []
        for ci in range(rows // c):
            rs = slice(ci * c, (ci + 1) * c)
            q = zc[rs, 0:d_a]
            f = zc[rs, d_a:2 * d_a]
            v = zc[rs, 2 * d_a:3 * d_a]
            g = zc[rs, 3 * d_a:4 * d_a]
            qs = _silu(q)
            half_th = fg_half * jnp.tanh(0.5 * f)
            forget = fg_mid + half_th
            kk = fg_half - half_th
            yield
            e = forget
            shift = 1
            while shift < n:
                e = jnp.where(row_in_sub >= shift, e * pltpu.roll(e, shift, axis=0), e)
                shift *= 2
                yield
            e_inv = jnp.minimum(1.0 / e, INV_DECAY_MAX)
            q_loc = qs * e
            k_loc = kk * e_inv
            p = [e[(i + 1) * n - 1:(i + 1) * n] for i in range(nb)]
            before = [None] * nb
            for i in range(1, nb):
                before[i] = p[i - 1] if i == 1 else before[i - 1] * p[i - 1]
            to_end = [None] * nb
            for i in reversed(range(nb)):
                to_end[i] = p[i] if i == nb - 1 else to_end[i + 1] * p[i]
            chunk_decay = to_end[0]
            yield
            q_dist = [[sub(q_loc, i) for i in range(nb)]]
            for d in range(1, nb):
                prev = q_dist[d - 1]
                q_dist.append([prev[i - (d - 1)] * p[i - d] for i in range(d, nb)])
            q_stack = jnp.concatenate([blk for grp in q_dist for blk in grp], axis=0).astype(BF16)
            q_state = jnp.concatenate(
                [sub(q_loc, 0)] + [sub(q_loc, i) * before[i] for i in range(1, nb)],
                axis=0).astype(BF16)
            k_state = jnp.concatenate(
                [sub(k_loc, i) * to_end[i] for i in range(nb)], axis=0).astype(BF16)
            k_loc_b = k_loc.astype(BF16)
            v_b = v.astype(BF16)
            gate_out = hgain * _silu(g)
            yield

            outs = []
            for hh in range(n_heads):
                sl = slice(hh * hd, (hh + 1) * hd)
                s_all = lax.dot_general(q_stack[:, sl], k_loc_b[:, sl], nt,
                                        preferred_element_type=F32)
                state_t = state_ref[hh]
                o_state = lax.dot_general(q_state[:, sl], state_t.astype(BF16), nt,
                                          preferred_element_type=F32)
                upd_t = lax.dot_general(v_b[:, sl], k_state[:, sl], tn,
                                        preferred_element_type=F32)
                yield
                scores = jnp.where(mask_diag, s_all[0:c], 0.0)
                r0 = c
                for d in range(1, nb):
                    grp = jnp.concatenate(
                        [jnp.zeros((d * n, c), F32), s_all[r0:r0 + (nb - d) * n]], axis=0)
                    scores = jnp.where(dist == d, grp, scores)
                    r0 += (nb - d) * n
                o_h = o_state + jnp.dot(scores.astype(BF16), v_b[:, sl],
                                        preferred_element_type=F32)
                state_ref[hh] = chunk_decay[:, sl] * state_t + upd_t
                yield
                ms = jnp.mean(o_h * o_h, axis=-1, keepdims=True)
                outs.append(o_h * lax.rsqrt(ms + EPS))
                yield
            chunks.append((jnp.concatenate(outs, axis=1) * gate_out).astype(BF16))
            yield
        out["a"] = w_out_part(jnp.concatenate(chunks, axis=0), 0)
        yield

    def lru_stream():
        xb = zc[:, zb0:zb0 + d_b]
        gate = zc[:, zb0 + d_b:zb0 + 2 * d_b]
        tail = xtail_ref[...]
        row8 = lax.broadcasted_iota(jnp.int32, (SUBLANES, d_b), 0)
        xc = convb_ref[...] + xb * convw_ref[CONV_WIDTH - 1:CONV_WIDTH, :]
        for k in range(1, CONV_WIDTH):
            rolled = pltpu.roll(xb, k, axis=0)
            head = jnp.where(row8 < k, pltpu.roll(tail, k, axis=0), rolled[0:SUBLANES])
            shifted = jnp.concatenate([head, rolled[SUBLANES:]], axis=0)
            xc = xc + shifted * convw_ref[CONV_WIDTH - 1 - k:CONV_WIDTH - k, :]
            yield
        xtail_ref[...] = xb[rows - SUBLANES:rows]

        gates = jnp.dot(xc.astype(BF16), wax_ref[...], preferred_element_type=F32) + bax_ref[...]
        yield
        r = _sigmoid(gates[:, 0:d_b])
        gate_in = _sigmoid(gates[:, d_b:2 * d_b])
        neg_lam = -lam_ref[...]
        softplus = jnp.maximum(neg_lam, 0.0) + jnp.log1p(jnp.exp(-jnp.abs(neg_lam)))
        log_a = (-LRU_C) * r * softplus
        yield
        a = jnp.exp(log_a)
        mult = _sqrt_nonneg(-jnp.tanh(log_a) * (a * a + 1.0))
        u = mult * gate_in * xc
        yield

        row_in_grp = lax.broadcasted_iota(jnp.int32, (rows, d_b), 0) % SUBLANES
        shift = 1
        while shift < SUBLANES:
            keep = row_in_grp >= shift
            u = jnp.where(keep, a * pltpu.roll(u, shift, axis=0) + u, u)
            a = jnp.where(keep, a * pltpu.roll(a, shift, axis=0), a)
            shift *= 2
            yield
        carry = hlru_ref[...]
        hs = []
        for gi in range(rows // SUBLANES):
            gs = slice(gi * SUBLANES, (gi + 1) * SUBLANES)
            hg = u[gs] + a[gs] * carry
            hs.append(hg)
            carry = hg[SUBLANES - 1:SUBLANES]
            if gi % 4 == 3:
                yield
        hlru_ref[...] = carry
        y = jnp.concatenate(hs, axis=0) * _gelu_tanh(gate)
        yield
        ms = _group_mean(y * y, g64)
        yield
        out_b = (y * lax.rsqrt(ms + EPS) * lnorm_ref[...]).astype(BF16)
        out["b"] = w_out_part(out_b, d_a)
        yield

    _interleave([hgrn_stream(), lru_stream(), sgu_stream(), projection_stream()],
                [4 * (n.bit_length() + 3 * n_heads + 3) + 1, 24, 16, d_in // PROJ_COLS + 1])
    o_ref[...] = h_ref[...] + out["a"] + out["b"] + out["c"]


def _interleave(streams, lengths):
    done = [0] * len(streams)
    live = list(range(len(streams)))
    while live:
        i = min(live, key=lambda j: (done[j] + 1) / lengths[j])
        try:
            next(streams[i])
            done[i] += 1
        except StopIteration:
            live.remove(i)


def _mixer_call(h2d, seq, layer, p):
    n_rows, d_model = h2d.shape
    n_tiles = n_rows // MIX_ROWS
    d_in = p["w_in"].shape[-1]
    d_a = p["hgrn_norm"].shape[-1]
    d_b = p["lru_norm"].shape[-1]
    d_c = p["sgu_norm"].shape[-1]
    d_mix = d_a + d_b + d_c
    depth = p["lb_logits"].shape[0]
    n_heads = d_a // HGRN_HEAD_DIM

    def per_layer(shape):
        nd = len(shape)
        return pl.BlockSpec((None,) + shape, lambda s: (layer,) + (0,) * nd)

    def whole(shape):
        nd = len(shape)
        return pl.BlockSpec(shape, lambda s: (0,) * nd)

    in_specs = [
        pl.BlockSpec((MIX_ROWS, d_model), lambda s: (s, 0)),
        pl.BlockSpec((MIX_ROWS, d_model), lambda s: (jnp.minimum(s + 1, n_tiles - 1), 0)),
        per_layer((1, d_model)),
        per_layer((d_model, d_in)),
        whole((depth, d_a)),
        per_layer((1, d_a)),
        per_layer((CONV_WIDTH, d_b)),
        per_layer((1, d_b)),
        per_layer((d_b, 2 * d_b)),
        per_layer((1, 2 * d_b)),
        per_layer((1, d_b)),
        per_layer((1, d_b)),
        per_layer((SGU_GROUPS, SGU_CHUNK, SGU_CHUNK)),
        per_layer((SGU_CHUNK, d_c)),
        per_layer((1, d_c)),
        per_layer((d_mix, d_model)),
        whole((d_b, d_b)),
    ]
    return pl.pallas_call(
        functools.partial(_mixer_kernel, layer=layer, tiles_per_seq=seq // MIX_ROWS),
        grid=(n_tiles,),
        in_specs=in_specs,
        out_specs=pl.BlockSpec((MIX_ROWS, d_model), lambda s: (s, 0)),
        out_shape=jax.ShapeDtypeStruct(h2d.shape, F32),
        scratch_shapes=[
            pltpu.VMEM((MIX_ROWS, d_in), F32),
            pltpu.VMEM((MIX_ROWS, d_in), F32),
            pltpu.VMEM((n_heads, HGRN_HEAD_DIM, HGRN_HEAD_DIM), F32),
            pltpu.VMEM((1, d_b), F32),
            pltpu.VMEM((SUBLANES, d_b), F32),
        ],
        compiler_params=pltpu.CompilerParams(
            dimension_semantics=("arbitrary",), vmem_limit_bytes=VMEM_LIMIT_BYTES),
        name="mixer",
    )(h2d, h2d, p["mix_norm"], p["w_in"], p["lb_logits"], p["hgrn_norm"], p["conv_w"], p["conv_b"],
      p["wax"], p["bax"], p["lam"], p["lru_norm"], p["sgu_w"], p["sgu_b"], p["sgu_norm"],
      p["w_out"], p["g64"])


def _block_diag(w):
    depth, nblk, d, _ = w.shape
    eye = jnp.eye(nblk, dtype=w.dtype)
    return jnp.einsum("lhij,hg->lhigj", w, eye).reshape(depth, nblk * d, nblk * d)


def kernel(x, ffn1_norm, ffn1_wg, ffn1_wu, ffn1_wd, mix_norm, w_in, hgrn_lb_logits, hgrn_norm,
           conv_w, conv_b, lru_wa, lru_ba, lru_wx, lru_bx, lru_lambda, lru_norm,
           sgu_w, sgu_b, sgu_norm, w_out, ffn2_norm, ffn2_wg, ffn2_wu, ffn2_wd, final_norm):
    bsz, seq, d_model = x.shape
    depth = w_in.shape[0]
    d_b = lru_norm.shape[-1]
    d_c = sgu_norm.shape[-1]
    assert seq % MIX_ROWS == 0 and (bsz * seq) % FFN_ROWS == 0
    assert MIX_ROWS % HGRN_CHUNK == 0 and MIX_ROWS % SGU_CHUNK == 0
    assert hgrn_norm.shape[-1] % HGRN_HEAD_DIM == 0

    row = lambda a: a.reshape(depth, 1, a.shape[-1])
    group_id = np.arange(d_b) // (d_b // LRU_BLOCKS)
    g64 = (group_id[:, None] == group_id[None, :]).astype(np.float32) / (d_b // LRU_BLOCKS)
    assert d_c == d_b and SGU_GROUPS == LRU_BLOCKS
    p = {
        "mix_norm": row(mix_norm),
        "w_in": w_in.astype(BF16),
        "lb_logits": hgrn_lb_logits,
        "hgrn_norm": row(hgrn_norm),
        "conv_w": conv_w,
        "conv_b": row(conv_b),
        "wax": jnp.concatenate([_block_diag(lru_wa), _block_diag(lru_wx)], axis=-1).astype(BF16),
        "bax": jnp.concatenate([lru_ba.reshape(depth, 1, d_b), lru_bx.reshape(depth, 1, d_b)], axis=-1),
        "lam": row(lru_lambda),
        "lru_norm": row(lru_norm),
        "sgu_w": sgu_w,
        "sgu_b": jnp.repeat(jnp.swapaxes(sgu_b, 1, 2), d_c // SGU_GROUPS, axis=2),
        "sgu_norm": row(sgu_norm),
        "w_out": w_out.astype(BF16),
        "g64": jnp.asarray(g64, BF16),
    }
    ffn1 = (row(ffn1_norm), ffn1_wg.astype(BF16), ffn1_wu.astype(BF16), ffn1_wd.astype(BF16))
    ffn2 = (row(ffn2_norm), ffn2_wg.astype(BF16), ffn2_wu.astype(BF16), ffn2_wd.astype(BF16))
    fgain = final_norm.reshape(1, d_model)

    h = x.reshape(bsz * seq, d_model)
    for layer in range(depth):
        h = _ffn_call(h, layer, *ffn1, fgain, False)
        h = _mixer_call(h, seq, layer, p)
        h = _ffn_call(h, layer, *ffn2, fgain, layer == depth - 1)
    return h.reshape(bsz, seq, d_model)
```

```python
import functools

import numpy as np
import jax
import jax.numpy as jnp
from jax import lax
from jax.experimental import pallas as pl
from jax.experimental.pallas import tpu as pltpu

F32 = jnp.float32
BF16 = jnp.bfloat16

EPS = 1e-6
LRU_C = 8.0
CONV_WIDTH = 4

HGRN_HEAD_DIM = 128
LRU_BLOCKS = 4
SGU_GROUPS = 4
SGU_CHUNK = 128

SUBLANES = 8
FFN_ROWS = 512
MIX_ROWS = 512
PROJ_COLS = 256
HGRN_CHUNK = 64
HGRN_SUB = 16
INV_DECAY_MAX = float(np.exp(80.0))
VMEM_LIMIT_BYTES = 56 * 1024 * 1024


def _sigmoid(x):
    return 0.5 * jnp.tanh(0.5 * x) + 0.5


def _silu(x):
    h = 0.5 * x
    return h * jnp.tanh(h) + h


def _gelu_tanh(x):
    c = float(np.sqrt(2.0 / np.pi))
    h = 0.5 * x
    return h * jnp.tanh(x * (c + (c * 0.044715) * (x * x))) + h


def _sqrt_nonneg(w):
    return w * lax.rsqrt(jnp.maximum(w, float(np.finfo(np.float32).tiny)))


def _rmsnorm(x, gain):
    ms = jnp.mean(x * x, axis=-1, keepdims=True)
    return x * lax.rsqrt(ms + EPS) * gain


def _split2(x):
    hi = x.astype(BF16)
    lo = (x - hi.astype(F32)).astype(BF16)
    return hi, lo


def _group_mean(x, gmat):
    hi, lo = _split2(x)
    return (jnp.dot(hi, gmat, preferred_element_type=F32)
            + jnp.dot(lo, gmat, preferred_element_type=F32))


def _ffn_kernel(x_ref, gain_ref, wg_ref, wu_ref, wd_ref, fgain_ref, o_ref, *, final_norm):
    x = x_ref[...]
    xn = _rmsnorm(x, gain_ref[...]).astype(BF16)
    g = jnp.dot(xn, wg_ref[...], preferred_element_type=F32)
    u = jnp.dot(xn, wu_ref[...], preferred_element_type=F32)
    a = (_silu(g) * u).astype(BF16)
    y = x + 0.5 * jnp.dot(a, wd_ref[...], preferred_element_type=F32)
    if final_norm:
        y = _rmsnorm(y, fgain_ref[...])
    o_ref[...] = y


def _ffn_call(h2d, layer, gain, wg, wu, wd, fgain, final_norm):
    n_rows, d_model = h2d.shape
    d_ff = wg.shape[-1]
    resident = pl.Buffered(1)
    return pl.pallas_call(
        functools.partial(_ffn_kernel, final_norm=final_norm),
        grid=(n_rows // FFN_ROWS,),
        in_specs=[
            pl.BlockSpec((FFN_ROWS, d_model), lambda i: (i, 0)),
            pl.BlockSpec((None, 1, d_model), lambda i: (layer, 0, 0)),
            pl.BlockSpec((None, d_model, d_ff), lambda i: (layer, 0, 0), pipeline_mode=resident),
            pl.BlockSpec((None, d_model, d_ff), lambda i: (layer, 0, 0), pipeline_mode=resident),
            pl.BlockSpec((None, d_ff, d_model), lambda i: (layer, 0, 0), pipeline_mode=resident),
            pl.BlockSpec((1, d_model), lambda i: (0, 0)),
        ],
        out_specs=pl.BlockSpec((FFN_ROWS, d_model), lambda i: (i, 0)),
        out_shape=jax.ShapeDtypeStruct(h2d.shape, F32),
        compiler_params=pltpu.CompilerParams(
            dimension_semantics=("arbitrary",), vmem_limit_bytes=VMEM_LIMIT_BYTES),
        name="ffn",
    )(h2d, gain, wg, wu, wd, fgain)


def _project_stream(x_ref, gain_ref, win_ref, z_ref):
    hn = _rmsnorm(x_ref[...], gain_ref[...]).astype(BF16)
    yield
    for j in range(win_ref.shape[-1] // PROJ_COLS):
        cs = slice(j * PROJ_COLS, (j + 1) * PROJ_COLS)
        z_ref[:, cs] = jnp.dot(hn, win_ref[:, cs], preferred_element_type=F32)
        yield


def _mixer_kernel(h_ref, hnext_ref, gain_ref, win_ref, *refs, layer, tiles_per_seq):
    z0_ref, z1_ref, state_ref, hlru_ref, xtail_ref = refs[-5:]
    step = pl.program_id(0)

    @pl.when(step % tiles_per_seq == 0)
    def _reset_state():
        state_ref[...] = jnp.zeros_like(state_ref)
        hlru_ref[...] = jnp.zeros_like(hlru_ref)
        xtail_ref[...] = jnp.zeros_like(xtail_ref)

    @pl.when(step == 0)
    def _first_projection():
        for _ in _project_stream(h_ref, gain_ref, win_ref, z0_ref):
            pass

    def tile(zc, zn):
        ahead = _project_stream(hnext_ref, gain_ref, win_ref, zn)
        _mixer_tile(*refs[:-5], state_ref, hlru_ref, xtail_ref, h_ref, zc, ahead, layer=layer)

    pl.when(step % 2 == 0)(functools.partial(tile, z0_ref, z1_ref))
    pl.when(step % 2 == 1)(functools.partial(tile, z1_ref, z0_ref))


def _mixer_tile(lbl_ref, hnorm_ref, convw_ref, convb_ref,
                wax_ref, bax_ref, lam_ref, lnorm_ref, sguw_ref, sgub_ref, snorm_ref,
                wout_ref, g64_ref, o_ref, state_ref, hlru_ref, xtail_ref,
                h_ref, zc, ahead_stream, *, layer):
    rows = h_ref.shape[0]
    d_a = hnorm_ref.shape[-1]
    d_b = lnorm_ref.shape[-1]
    d_c = snorm_ref.shape[-1]
    n_heads = d_a // HGRN_HEAD_DIM
    hd = HGRN_HEAD_DIM
    c, n = HGRN_CHUNK, HGRN_SUB
    nb = c // n
    zb0 = 4 * d_a
    zc0 = zb0 + 2 * d_b

    g64 = g64_ref[...]
    out = {}

    def w_out_part(stage_out, col0):
        return jnp.dot(stage_out, wout_ref[col0:col0 + stage_out.shape[1], :],
                       preferred_element_type=F32)


    def sgu_stream():
        tri = (lax.broadcasted_iota(jnp.int32, (SGU_CHUNK, SGU_CHUNK), 1)
               <= lax.broadcasted_iota(jnp.int32, (SGU_CHUNK, SGU_CHUNK), 0))
        gdim = d_c // SGU_GROUPS
        w_causal = [jnp.where(tri, sguw_ref[gi], 0.0).astype(BF16) for gi in range(SGU_GROUPS)]
        yield
        outs = []
        for ci in range(rows // SGU_CHUNK):
            rs = slice(ci * SGU_CHUNK, (ci + 1) * SGU_CHUNK)
            su = _gelu_tanh(zc[rs, zc0:zc0 + d_c])
            sv = _gelu_tanh(zc[rs, zc0 + d_c:zc0 + 2 * d_c])
            yield
            mu = _group_mean(sv, g64)
            yield
            dv = sv - mu
            var = _group_mean(dv * dv, g64)
            yield
            vn = (dv * lax.rsqrt(var + EPS)).astype(BF16)
            zs = [jnp.dot(w_causal[gi], vn[:, gi * gdim:(gi + 1) * gdim],
                          preferred_element_type=F32) for gi in range(SGU_GROUPS)]
            yield
            ys = su * (jnp.concatenate(zs, axis=1) + sgub_ref[...])
            ms = _group_mean(ys * ys, g64)
            yield
            outs.append((ys * lax.rsqrt(ms + EPS) * snorm_ref[...]).astype(BF16))
            yield
        out["c"] = w_out_part(jnp.concatenate(outs, axis=0), d_a + d_b)
        yield

    def hgrn_stream():
        logits = lbl_ref[...]
        lexp = jnp.exp(logits - jnp.max(logits, axis=0, keepdims=True))
        if layer == 0:
            lb = jnp.zeros((1, d_a), F32)
        else:
            lb = (jnp.sum(lexp[1:layer + 1], axis=0, keepdims=True)
                  / jnp.sum(lexp, axis=0, keepdims=True))
        fg_half = 0.5 * (1.0 - lb)
        fg_mid = lb + fg_half

        row_c = lax.broadcasted_iota(jnp.int32, (c, c), 0)
        col_c = lax.broadcasted_iota(jnp.int32, (c, c), 1)
        dist = row_c // n - col_c // n
        mask_diag = (dist == 0) & (col_c <= row_c)
        row_in_sub = lax.broadcasted_iota(jnp.int32, (c, d_a), 0) % n
        hgain = hnorm_ref[...]
        nt = (((1,), (1,)), ((), ()))
        tn = (((0,), (0,)), ((), ()))

        def sub(a, i):
            return a[i * n:(i + 1) * n]

        chunks = []
        for ci in range(rows // c):
            rs = slice(ci * c, (ci + 1) * c)
            q = zc[rs, 0:d_a]
            f = zc[rs, d_a:2 * d_a]
            v = zc[rs, 2 * d_a:3 * d_a]
            g = zc[rs, 3 * d_a:4 * d_a]
            qs = _silu(q)
            half_th = fg_half * jnp.tanh(0.5 * f)
            forget = fg_mid + half_th
            kk = fg_half - half_th
            yield
            e = forget
            shift = 1
            while shift < n:
                e = jnp.where(row_in_sub >= shift, e * pltpu.roll(e, shift, axis=0), e)
                shift *= 2
                yield
            e_inv = jnp.minimum(1.0 / e, INV_DECAY_MAX)
            q_loc = qs * e
            k_loc = kk * e_inv
            p = [e[(i + 1) * n - 1:(i + 1) * n] for i in range(nb)]
            before = [None] * nb
            for i in range(1, nb):
                before[i] = p[i - 1] if i == 1 else before[i - 1] * p[i - 1]
            to_end = [None] * nb
            for i in reversed(range(nb)):
                to_end[i] = p[i] if i == nb - 1 else to_end[i + 1] * p[i]
            chunk_decay = to_end[0]
            yield
            q_dist = [[sub(q_loc, i) for i in range(nb)]]
            for d in range(1, nb):
                prev = q_dist[d - 1]
                q_dist.append([prev[i - (d - 1)] * p[i - d] for i in range(d, nb)])
            q_stack = jnp.concatenate([blk for grp in q_dist for blk in grp], axis=0).astype(BF16)
            q_state = jnp.concatenate(
                [sub(q_loc, 0)] + [sub(q_loc, i) * before[i] for i in range(1, nb)],
                axis=0).astype(BF16)
            k_state = jnp.concatenate(
                [sub(k_loc, i) * to_end[i] for i in range(nb)], axis=0).astype(BF16)
            k_loc_b = k_loc.astype(BF16)
            v_b = v.astype(BF16)
            gate_out = hgain * _silu(g)
            yield

            outs = []
            for hh in range(n_heads):
                sl = slice(hh * hd, (hh + 1) * hd)
                s_all = lax.dot_general(q_stack[:, sl], k_loc_b[:, sl], nt,
                                        preferred_element_type=F32)
                state_t = state_ref[hh]
                o_state = lax.dot_general(q_state[:, sl], state_t.astype(BF16), nt,
                                          preferred_element_type=F32)
                upd_t = lax.dot_general(v_b[:, sl], k_state[:, sl], tn,
                                        preferred_element_type=F32)
                yield
                scores = jnp.where(mask_diag, s_all[0:c], 0.0)
                r0 = c
                for d in range(1, nb):
                    grp = jnp.concatenate(
                        [jnp.zeros((d * n, c), F32), s_all[r0:r0 + (nb - d) * n]], axis=0)
                    scores = jnp.where(dist == d, grp, scores)
                    r0 += (nb - d) * n
                o_h = o_state + jnp.dot(scores.astype(BF16), v_b[:, sl],
                                        preferred_element_type=F32)
                state_ref[hh] = chunk_decay[:, sl] * state_t + upd_t
                yield
                ms = jnp.mean(o_h * o_h, axis=-1, keepdims=True)
                outs.append(o_h * lax.rsqrt(ms + EPS))
                yield
            chunks.append((jnp.concatenate(outs, axis=1) * gate_out).astype(BF16))
            yield
        out["a"] = w_out_part(jnp.concatenate(chunks, axis=0), 0)
        yield

    def lru_stream():
        xb = zc[:, zb0:zb0 + d_b]
        gate = zc[:, zb0 + d_b:zb0 + 2 * d_b]
        tail = xtail_ref[...]
        row8 = lax.broadcasted_iota(jnp.int32, (SUBLANES, d_b), 0)
        xc = convb_ref[...] + xb * convw_ref[CONV_WIDTH - 1:CONV_WIDTH, :]
        for k in range(1, CONV_WIDTH):
            rolled = pltpu.roll(xb, k, axis=0)
            head = jnp.where(row8 < k, pltpu.roll(tail, k, axis=0), rolled[0:SUBLANES])
            shifted = jnp.concatenate([head, rolled[SUBLANES:]], axis=0)
            xc = xc + shifted * convw_ref[CONV_WIDTH - 1 - k:CONV_WIDTH - k, :]
            yield
        xtail_ref[...] = xb[rows - SUBLANES:rows]

        gates = jnp.dot(xc.astype(BF16), wax_ref[...], preferred_element_type=F32) + bax_ref[...]
        yield
        r = _sigmoid(gates[:, 0:d_b])
        gate_in = _sigmoid(gates[:, d_b:2 * d_b])
        neg_lam = -lam_ref[...]
        softplus = jnp.maximum(neg_lam, 0.0) + jnp.log1p(jnp.exp(-jnp.abs(neg_lam)))
        log_a = (-LRU_C) * r * softplus
        yield
        a = jnp.exp(log_a)
        mult = _sqrt_nonneg(-jnp.tanh(log_a) * (a * a + 1.0))
        u = mult * gate_in * xc
        yield

        row_in_grp = lax.broadcasted_iota(jnp.int32, (rows, d_b), 0) % SUBLANES
        shift = 1
        while shift < SUBLANES:
            keep = row_in_grp >= shift
            u = jnp.where(keep, a * pltpu.roll(u, shift, axis=0) + u, u)
            a = jnp.where(keep, a * pltpu.roll(a, shift, axis=0), a)
            shift *= 2
            yield
        carry = hlru_ref[...]
        hs = []
        for gi in range(rows // SUBLANES):
            gs = slice(gi * SUBLANES, (gi + 1) * SUBLANES)
            hg = u[gs] + a[gs] * carry
            hs.append(hg)
            carry = hg[SUBLANES - 1:SUBLANES]
            if gi % 4 == 3:
                yield
        hlru_ref[...] = carry
        y = jnp.concatenate(hs, axis=0) * _gelu_tanh(gate)
        yield
        ms = _group_mean(y * y, g64)
        yield
        out_b = (y * lax.rsqrt(ms + EPS) * lnorm_ref[...]).astype(BF16)
        out["b"] = w_out_part(out_b, d_a)
        yield

    _interleave([hgrn_stream(), lru_stream(), sgu_stream(), ahead_stream],
                [4 * (n.bit_length() + 3 * n_heads + 3) + 1, 24, 16,
                 zc.shape[-1] // PROJ_COLS + 1])
    o_ref[...] = h_ref[...] + out["a"] + out["b"] + out["c"]


def _interleave(streams, lengths):
    done = [0] * len(streams)
    live = list(range(len(streams)))
    while live:
        i = min(live, key=lambda j: (done[j] + 1) / lengths[j])
        try:
            next(streams[i])
            done[i] += 1
        except StopIteration:
            live.remove(i)


def _mixer_call(h2d, seq, layer, p):
    n_rows, d_model = h2d.shape
    n_tiles = n_rows // MIX_ROWS
    d_in = p["w_in"].shape[-1]
    d_a = p["hgrn_norm"].shape[-1]
    d_b = p["lru_norm"].shape[-1]
    d_c = p["sgu_norm"].shape[-1]
    d_mix = d_a + d_b + d_c
    depth = p["lb_logits"].shape[0]
    n_heads = d_a // HGRN_HEAD_DIM

    def per_layer(shape):
        nd = len(shape)
        return pl.BlockSpec((None,) + shape, lambda s: (layer,) + (0,) * nd,
                            pipeline_mode=pl.Buffered(1))

    def whole(shape):
        nd = len(shape)
        return pl.BlockSpec(shape, lambda s: (0,) * nd)

    in_specs = [
        pl.BlockSpec((MIX_ROWS, d_model), lambda s: (s, 0)),
        pl.BlockSpec((MIX_ROWS, d_model), lambda s: (jnp.minimum(s + 1, n_tiles - 1), 0)),
        per_layer((1, d_model)),
        per_layer((d_model, d_in)),
        whole((depth, d_a)),
        per_layer((1, d_a)),
        per_layer((CONV_WIDTH, d_b)),
        per_layer((1, d_b)),
        per_layer((d_b, 2 * d_b)),
        per_layer((1, 2 * d_b)),
        per_layer((1, d_b)),
        per_layer((1, d_b)),
        per_layer((SGU_GROUPS, SGU_CHUNK, SGU_CHUNK)),
        per_layer((SGU_CHUNK, d_c)),
        per_layer((1, d_c)),
        per_layer((d_mix, d_model)),
        whole((d_b, d_b)),
    ]
    return pl.pallas_call(
        functools.partial(_mixer_kernel, layer=layer, tiles_per_seq=seq // MIX_ROWS),
        grid=(n_tiles,),
        in_specs=in_specs,
        out_specs=pl.BlockSpec((MIX_ROWS, d_model), lambda s: (s, 0)),
        out_shape=jax.ShapeDtypeStruct(h2d.shape, F32),
        scratch_shapes=[
            pltpu.VMEM((MIX_ROWS, d_in), F32),
            pltpu.VMEM((MIX_ROWS, d_in), F32),
            pltpu.VMEM((n_heads, HGRN_HEAD_DIM, HGRN_HEAD_DIM), F32),
            pltpu.VMEM((1, d_b), F32),
            pltpu.VMEM((SUBLANES, d_b), F32),
        ],
        compiler_params=pltpu.CompilerParams(
            dimension_semantics=("arbitrary",), vmem_limit_bytes=VMEM_LIMIT_BYTES),
        name="mixer",
    )(h2d, h2d, p["mix_norm"], p["w_in"], p["lb_logits"], p["hgrn_norm"], p["conv_w"], p["conv_b"],
      p["wax"], p["bax"], p["lam"], p["lru_norm"], p["sgu_w"], p["sgu_b"], p["sgu_norm"],
      p["w_out"], p["g64"])


def _block_diag(w):
    depth, nblk, d, _ = w.shape
    eye = jnp.eye(nblk, dtype=w.dtype)
    return jnp.einsum("lhij,hg->lhigj", w, eye).reshape(depth, nblk * d, nblk * d)


def kernel(x, ffn1_norm, ffn1_wg, ffn1_wu, ffn1_wd, mix_norm, w_in, hgrn_lb_logits, hgrn_norm,
           conv_w, conv_b, lru_wa, lru_ba, lru_wx, lru_bx, lru_lambda, lru_norm,
           sgu_w, sgu_b, sgu_norm, w_out, ffn2_norm, ffn2_wg, ffn2_wu, ffn2_wd, final_norm):
    bsz, seq, d_model = x.shape
    depth = w_in.shape[0]
    d_b = lru_norm.shape[-1]
    d_c = sgu_norm.shape[-1]
    assert seq % MIX_ROWS == 0 and (bsz * seq) % FFN_ROWS == 0
    assert MIX_ROWS % HGRN_CHUNK == 0 and MIX_ROWS % SGU_CHUNK == 0
    assert hgrn_norm.shape[-1] % HGRN_HEAD_DIM == 0

    row = lambda a: a.reshape(depth, 1, a.shape[-1])
    group_id = np.arange(d_b) // (d_b // LRU_BLOCKS)
    g64 = (group_id[:, None] == group_id[None, :]).astype(np.float32) / (d_b // LRU_BLOCKS)
    assert d_c == d_b and SGU_GROUPS == LRU_BLOCKS
    p = {
        "mix_norm": row(mix_norm),
        "w_in": w_in.astype(BF16),
        "lb_logits": hgrn_lb_logits,
        "hgrn_norm": row(hgrn_norm),
        "conv_w": conv_w,
        "conv_b": row(conv_b),
        "wax": jnp.concatenate([_block_diag(lru_wa), _block_diag(lru_wx)], axis=-1).astype(BF16),
        "bax": jnp.concatenate([lru_ba.reshape(depth, 1, d_b), lru_bx.reshape(depth, 1, d_b)], axis=-1),
        "lam": row(lru_lambda),
        "lru_norm": row(lru_norm),
        "sgu_w": sgu_w,
        "sgu_b": jnp.repeat(jnp.swapaxes(sgu_b, 1, 2), d_c // SGU_GROUPS, axis=2),
        "sgu_norm": row(sgu_norm),
        "w_out": w_out.astype(BF16),
        "g64": jnp.asarray(g64, BF16),
    }
    ffn1 = (row(ffn1_norm), ffn1_wg.astype(BF16), ffn1_wu.astype(BF16), ffn1_wd.astype(BF16))
    ffn2 = (row(ffn2_norm), ffn2_wg.astype(BF16), ffn2_wu.astype(BF16), ffn2_wd.astype(BF16))
    fgain = final_norm.reshape(1, d_model)

    h = x.reshape(bsz * seq, d_model)
    for layer in range(depth):
        h = _ffn_call(h, layer, *ffn1, fgain, False)
        h = _mixer_call(h, seq, layer, p)
        h = _ffn_call(h, layer, *ffn2, fgain, layer == depth - 1)
    return h.reshape(bsz, seq, d_model)
```

```python
import functools

import numpy as np
import jax
import jax.numpy as jnp
from jax import lax
from jax.experimental import pallas as pl
from jax.experimental.pallas import tpu as pltpu

F32 = jnp.float32
BF16 = jnp.bfloat16

EPS = 1e-6
LRU_C = 8.0
CONV_WIDTH = 4

HGRN_HEAD_DIM = 128
LRU_BLOCKS = 4
SGU_GROUPS = 4
SGU_CHUNK = 128

SUBLANES = 8
FFN_ROWS = 512
MIX_ROWS = 512
PROJ_COLS = 256
HGRN_PACE = 41
LRU_PACE = 24
SGU_PACE = 16
HGRN_CHUNK = 128
HGRN_SUB = 16
INV_DECAY_MAX = float(np.exp(80.0))
VMEM_LIMIT_BYTES = 56 * 1024 * 1024


def _sigmoid(x):
    return 0.5 * jnp.tanh(0.5 * x) + 0.5


def _silu(x):
    h = 0.5 * x
    return h * jnp.tanh(h) + h


def _gelu_tanh(x):
    c = float(np.sqrt(2.0 / np.pi))
    h = 0.5 * x
    return h * jnp.tanh(x * (c + (c * 0.044715) * (x * x))) + h


def _sqrt_nonneg(w):
    return w * lax.rsqrt(jnp.maximum(w, float(np.finfo(np.float32).tiny)))


def _rmsnorm(x, gain):
    ms = jnp.mean(x * x, axis=-1, keepdims=True)
    return x * lax.rsqrt(ms + EPS) * gain


def _split2(x):
    hi = x.astype(BF16)
    lo = (x - hi.astype(F32)).astype(BF16)
    return hi, lo


def _group_mean(x, gmat):
    hi, lo = _split2(x)
    return (jnp.dot(hi, gmat, preferred_element_type=F32)
            + jnp.dot(lo, gmat, preferred_element_type=F32))


def _ffn_kernel(x_ref, gain_ref, wg_ref, wu_ref, wd_ref, fgain_ref, o_ref, *, final_norm):
    x = x_ref[...]
    xn = _rmsnorm(x, gain_ref[...]).astype(BF16)
    g = jnp.dot(xn, wg_ref[...], preferred_element_type=F32)
    u = jnp.dot(xn, wu_ref[...], preferred_element_type=F32)
    a = (_silu(g) * u).astype(BF16)
    y = x + 0.5 * jnp.dot(a, wd_ref[...], preferred_element_type=F32)
    if final_norm:
        y = _rmsnorm(y, fgain_ref[...])
    o_ref[...] = y


def _ffn_call(h2d, layer, gain, wg, wu, wd, fgain, final_norm):
    n_rows, d_model = h2d.shape
    d_ff = wg.shape[-1]
    resident = pl.Buffered(1)
    return pl.pallas_call(
        functools.partial(_ffn_kernel, final_norm=final_norm),
        grid=(n_rows // FFN_ROWS,),
        in_specs=[
            pl.BlockSpec((FFN_ROWS, d_model), lambda i: (i, 0)),
            pl.BlockSpec((None, 1, d_model), lambda i: (layer, 0, 0)),
            pl.BlockSpec((None, d_model, d_ff), lambda i: (layer, 0, 0), pipeline_mode=resident),
            pl.BlockSpec((None, d_model, d_ff), lambda i: (layer, 0, 0), pipeline_mode=resident),
            pl.BlockSpec((None, d_ff, d_model), lambda i: (layer, 0, 0), pipeline_mode=resident),
            pl.BlockSpec((1, d_model), lambda i: (0, 0)),
        ],
        out_specs=pl.BlockSpec((FFN_ROWS, d_model), lambda i: (i, 0)),
        out_shape=jax.ShapeDtypeStruct(h2d.shape, F32),
        compiler_params=pltpu.CompilerParams(
            dimension_semantics=("arbitrary",), vmem_limit_bytes=VMEM_LIMIT_BYTES),
        name="ffn",
    )(h2d, gain, wg, wu, wd, fgain)


def _project_stream(x_ref, gain_ref, win_ref, z_ref):
    hn = _rmsnorm(x_ref[...], gain_ref[...]).astype(BF16)
    yield
    for j in range(win_ref.shape[-1] // PROJ_COLS):
        cs = slice(j * PROJ_COLS, (j + 1) * PROJ_COLS)
        z_ref[:, cs] = jnp.dot(hn, win_ref[:, cs], preferred_element_type=F32)
        yield


def _mixer_kernel(h_ref, hnext_ref, gain_ref, win_ref, *refs, layer, tiles_per_seq):
    z0_ref, z1_ref, state_ref, hlru_ref, xtail_ref = refs[-5:]
    step = pl.program_id(0)

    @pl.when(step % tiles_per_seq == 0)
    def _reset_state():
        state_ref[...] = jnp.zeros_like(state_ref)
        hlru_ref[...] = jnp.zeros_like(hlru_ref)
        xtail_ref[...] = jnp.zeros_like(xtail_ref)

    @pl.when(step == 0)
    def _first_projection():
        for _ in _project_stream(h_ref, gain_ref, win_ref, z0_ref):
            pass

    def tile(zc, zn):
        ahead = _project_stream(hnext_ref, gain_ref, win_ref, zn)
        _mixer_tile(*refs[:-5], state_ref, hlru_ref, xtail_ref, h_ref, zc, ahead, layer=layer)

    pl.when(step % 2 == 0)(functools.partial(tile, z0_ref, z1_ref))
    pl.when(step % 2 == 1)(functools.partial(tile, z1_ref, z0_ref))


def _mixer_tile(lbl_ref, hnorm_ref, convw_ref, convb_ref,
                wax_ref, bax_ref, lam_ref, lnorm_ref, sguw_ref, sgub_ref, snorm_ref,
                wout_ref, g64_ref, o_ref, state_ref, hlru_ref, xtail_ref,
                h_ref, zc, ahead_stream, *, layer):
    rows = h_ref.shape[0]
    d_a = hnorm_ref.shape[-1]
    d_b = lnorm_ref.shape[-1]
    d_c = snorm_ref.shape[-1]
    n_heads = d_a // HGRN_HEAD_DIM
    hd = HGRN_HEAD_DIM
    c, n = HGRN_CHUNK, HGRN_SUB
    nb = c // n
    zb0 = 4 * d_a
    zc0 = zb0 + 2 * d_b

    g64 = g64_ref[...]
    out = {}

    def w_out_part(stage_out, col0):
        return jnp.dot(stage_out, wout_ref[col0:col0 + stage_out.shape[1], :],
                       preferred_element_type=F32)


    def sgu_stream():
        tri = (lax.broadcasted_iota(jnp.int32, (SGU_CHUNK, SGU_CHUNK), 1)
               <= lax.broadcasted_iota(jnp.int32, (SGU_CHUNK, SGU_CHUNK), 0))
        gdim = d_c // SGU_GROUPS
        w_causal = [jnp.where(tri, sguw_ref[gi], 0.0).astype(BF16) for gi in range(SGU_GROUPS)]
        yield
        outs = []
        for ci in range(rows // SGU_CHUNK):
            rs = slice(ci * SGU_CHUNK, (ci + 1) * SGU_CHUNK)
            su = _gelu_tanh(zc[rs, zc0:zc0 + d_c])
            sv = _gelu_tanh(zc[rs, zc0 + d_c:zc0 + 2 * d_c])
            yield
            mu = _group_mean(sv, g64)
            yield
            dv = sv - mu
            var = _group_mean(dv * dv, g64)
            yield
            vn = (dv * lax.rsqrt(var + EPS)).astype(BF16)
            zs = [jnp.dot(w_causal[gi], vn[:, gi * gdim:(gi + 1) * gdim],
                          preferred_element_type=F32) for gi in range(SGU_GROUPS)]
            yield
            ys = su * (jnp.concatenate(zs, axis=1) + sgub_ref[...])
            ms = _group_mean(ys * ys, g64)
            yield
            outs.append((ys * lax.rsqrt(ms + EPS) * snorm_ref[...]).astype(BF16))
            yield
        out["c"] = w_out_part(jnp.concatenate(outs, axis=0), d_a + d_b)
        yield

    def hgrn_stream():
        logits = lbl_ref[...]
        lexp = jnp.exp(logits - jnp.max(logits, axis=0, keepdims=True))
        if layer == 0:
            lb = jnp.zeros((1, d_a), F32)
        else:
            lb = (jnp.sum(lexp[1:layer + 1], axis=0, keepdims=True)
                  / jnp.sum(lexp, axis=0, keepdims=True))
        fg_half = 0.5 * (1.0 - lb)
        fg_mid = lb + fg_half

        row_c = lax.broadcasted_iota(jnp.int32, (c, c), 0)
        col_c = lax.broadcasted_iota(jnp.int32, (c, c), 1)
        dist = row_c // n - col_c // n
        mask_diag = (dist == 0) & (col_c <= row_c)
        row_in_sub = lax.broadcasted_iota(jnp.int32, (c, d_a), 0) % n
        hgain = hnorm_ref[...]
        nt = (((1,), (1,)), ((), ()))
        tn = (((0,), (0,)), ((), ()))

        def sub(a, i):
            return a[i * n:(i + 1) * n]

        chunks = []
        for ci in range(rows // c):
            rs = slice(ci * c, (ci + 1) * c)
            q = zc[rs, 0:d_a]
            f = zc[rs, d_a:2 * d_a]
            v = zc[rs, 2 * d_a:3 * d_a]
            g = zc[rs, 3 * d_a:4 * d_a]
            qs = _silu(q)
            half_th = fg_half * jnp.tanh(0.5 * f)
            forget = fg_mid + half_th
            kk = fg_half - half_th
            yield
            e = forget
            shift = 1
            while shift < n:
                e = jnp.where(row_in_sub >= shift, e * pltpu.roll(e, shift, axis=0), e)
                shift *= 2
                yield
            e_inv = jnp.minimum(1.0 / e, INV_DECAY_MAX)
            q_loc = qs * e
            k_loc = kk * e_inv
            p = [e[(i + 1) * n - 1:(i + 1) * n] for i in range(nb)]
            before = [None] * nb
            for i in range(1, nb):
                before[i] = p[i - 1] if i == 1 else before[i - 1] * p[i - 1]
            to_end = [None] * nb
            for i in reversed(range(nb)):
                to_end[i] = p[i] if i == nb - 1 else to_end[i + 1] * p[i]
            chunk_decay = to_end[0]
            yield
            q_dist = [[sub(q_loc, i) for i in range(nb)]]
            for d in range(1, nb):
                prev = q_dist[d - 1]
                q_dist.append([prev[i - (d - 1)] * p[i - d] for i in range(d, nb)])
            q_stack = jnp.concatenate([blk for grp in q_dist for blk in grp], axis=0).astype(BF16)
            q_state = jnp.concatenate(
                [sub(q_loc, 0)] + [sub(q_loc, i) * before[i] for i in range(1, nb)],
                axis=0).astype(BF16)
            k_state = jnp.concatenate(
                [sub(k_loc, i) * to_end[i] for i in range(nb)], axis=0).astype(BF16)
            k_loc_b = k_loc.astype(BF16)
            v_b = v.astype(BF16)
            gate_out = hgain * _silu(g)
            yield

            outs = []
            for hh in range(n_heads):
                sl = slice(hh * hd, (hh + 1) * hd)
                s_all = lax.dot_general(q_stack[:, sl], k_loc_b[:, sl], nt,
                                        preferred_element_type=F32)
                state_t = state_ref[hh]
                o_state = lax.dot_general(q_state[:, sl], state_t.astype(BF16), nt,
                                          preferred_element_type=F32)
                upd_t = lax.dot_general(v_b[:, sl], k_state[:, sl], tn,
                                        preferred_element_type=F32)
                yield
                scores = jnp.where(mask_diag, s_all[0:c], 0.0)
                r0 = c
                for d in range(1, nb):
                    grp = jnp.concatenate(
                        [jnp.zeros((d * n, c), F32), s_all[r0:r0 + (nb - d) * n]], axis=0)
                    scores = jnp.where(dist == d, grp, scores)
                    r0 += (nb - d) * n
                o_h = o_state + jnp.dot(scores.astype(BF16), v_b[:, sl],
                                        preferred_element_type=F32)
                state_ref[hh] = chunk_decay[:, sl] * state_t + upd_t
                yield
                ms = jnp.mean(o_h * o_h, axis=-1, keepdims=True)
                outs.append(o_h * lax.rsqrt(ms + EPS))
                yield
            chunks.append((jnp.concatenate(outs, axis=1) * gate_out).astype(BF16))
            yield
        out["a"] = w_out_part(jnp.concatenate(chunks, axis=0), 0)
        yield

    def lru_stream():
        xb = zc[:, zb0:zb0 + d_b]
        gate = zc[:, zb0 + d_b:zb0 + 2 * d_b]
        tail = xtail_ref[...]
        row8 = lax.broadcasted_iota(jnp.int32, (SUBLANES, d_b), 0)
        xc = convb_ref[...] + xb * convw_ref[CONV_WIDTH - 1:CONV_WIDTH, :]
        for k in range(1, CONV_WIDTH):
            rolled = pltpu.roll(xb, k, axis=0)
            head = jnp.where(row8 < k, pltpu.roll(tail, k, axis=0), rolled[0:SUBLANES])
            shifted = jnp.concatenate([head, rolled[SUBLANES:]], axis=0)
            xc = xc + shifted * convw_ref[CONV_WIDTH - 1 - k:CONV_WIDTH - k, :]
            yield
        xtail_ref[...] = xb[rows - SUBLANES:rows]

        gates = jnp.dot(xc.astype(BF16), wax_ref[...], preferred_element_type=F32) + bax_ref[...]
        yield
        r = _sigmoid(gates[:, 0:d_b])
        gate_in = _sigmoid(gates[:, d_b:2 * d_b])
        neg_lam = -lam_ref[...]
        softplus = jnp.maximum(neg_lam, 0.0) + jnp.log1p(jnp.exp(-jnp.abs(neg_lam)))
        log_a = (-LRU_C) * r * softplus
        yield
        a = jnp.exp(log_a)
        mult = _sqrt_nonneg(-jnp.tanh(log_a) * (a * a + 1.0))
        u = mult * gate_in * xc
        yield

        row_in_grp = lax.broadcasted_iota(jnp.int32, (rows, d_b), 0) % SUBLANES
        shift = 1
        while shift < SUBLANES:
            keep = row_in_grp >= shift
            u = jnp.where(keep, a * pltpu.roll(u, shift, axis=0) + u, u)
            a = jnp.where(keep, a * pltpu.roll(a, shift, axis=0), a)
            shift *= 2
            yield
        carry = hlru_ref[...]
        hs = []
        for gi in range(rows // SUBLANES):
            gs = slice(gi * SUBLANES, (gi + 1) * SUBLANES)
            hg = u[gs] + a[gs] * carry
            hs.append(hg)
            carry = hg[SUBLANES - 1:SUBLANES]
            if gi % 4 == 3:
                yield
        hlru_ref[...] = carry
        y = jnp.concatenate(hs, axis=0) * _gelu_tanh(gate)
        yield
        ms = _group_mean(y * y, g64)
        yield
        out_b = (y * lax.rsqrt(ms + EPS) * lnorm_ref[...]).astype(BF16)
        out["b"] = w_out_part(out_b, d_a)
        yield

    _interleave([hgrn_stream(), lru_stream(), sgu_stream(), ahead_stream],
                [HGRN_PACE, LRU_PACE, SGU_PACE, zc.shape[-1] // PROJ_COLS + 1])
    o_ref[...] = h_ref[...] + out["a"] + out["b"] + out["c"]


def _interleave(streams, paces):
    done = [0] * len(streams)
    live = list(range(len(streams)))
    while live:
        i = min(live, key=lambda j: (done[j] + 1) / paces[j])
        try:
            next(streams[i])
            done[i] += 1
        except StopIteration:
            live.remove(i)


def _mixer_call(h2d, seq, layer, p):
    n_rows, d_model = h2d.shape
    n_tiles = n_rows // MIX_ROWS
    d_in = p["w_in"].shape[-1]
    d_a = p["hgrn_norm"].shape[-1]
    d_b = p["lru_norm"].shape[-1]
    d_c = p["sgu_norm"].shape[-1]
    d_mix = d_a + d_b + d_c
    depth = p["lb_logits"].shape[0]
    n_heads = d_a // HGRN_HEAD_DIM

    def per_layer(shape):
        nd = len(shape)
        return pl.BlockSpec((None,) + shape, lambda s: (layer,) + (0,) * nd,
                            pipeline_mode=pl.Buffered(1))

    def whole(shape):
        nd = len(shape)
        return pl.BlockSpec(shape, lambda s: (0,) * nd)

    in_specs = [
        pl.BlockSpec((MIX_ROWS, d_model), lambda s: (s, 0)),
        pl.BlockSpec((MIX_ROWS, d_model), lambda s: (jnp.minimum(s + 1, n_tiles - 1), 0)),
        per_layer((1, d_model)),
        per_layer((d_model, d_in)),
        whole((depth, d_a)),
        per_layer((1, d_a)),
        per_layer((CONV_WIDTH, d_b)),
        per_layer((1, d_b)),
        per_layer((d_b, 2 * d_b)),
        per_layer((1, 2 * d_b)),
        per_layer((1, d_b)),
        per_layer((1, d_b)),
        per_layer((SGU_GROUPS, SGU_CHUNK, SGU_CHUNK)),
        per_layer((SGU_CHUNK, d_c)),
        per_layer((1, d_c)),
        per_layer((d_mix, d_model)),
        whole((d_b, d_b)),
    ]
    return pl.pallas_call(
        functools.partial(_mixer_kernel, layer=layer, tiles_per_seq=seq // MIX_ROWS),
        grid=(n_tiles,),
        in_specs=in_specs,
        out_specs=pl.BlockSpec((MIX_ROWS, d_model), lambda s: (s, 0)),
        out_shape=jax.ShapeDtypeStruct(h2d.shape, F32),
        scratch_shapes=[
            pltpu.VMEM((MIX_ROWS, d_in), F32),
            pltpu.VMEM((MIX_ROWS, d_in), F32),
            pltpu.VMEM((n_heads, HGRN_HEAD_DIM, HGRN_HEAD_DIM), F32),
            pltpu.VMEM((1, d_b), F32),
            pltpu.VMEM((SUBLANES, d_b), F32),
        ],
        compiler_params=pltpu.CompilerParams(
            dimension_semantics=("arbitrary",), vmem_limit_bytes=VMEM_LIMIT_BYTES),
        name="mixer",
    )(h2d, h2d, p["mix_norm"], p["w_in"], p["lb_logits"], p["hgrn_norm"], p["conv_w"], p["conv_b"],
      p["wax"], p["bax"], p["lam"], p["lru_norm"], p["sgu_w"], p["sgu_b"], p["sgu_norm"],
      p["w_out"], p["g64"])


def _block_diag(w):
    depth, nblk, d, _ = w.shape
    eye = jnp.eye(nblk, dtype=w.dtype)
    return jnp.einsum("lhij,hg->lhigj", w, eye).reshape(depth, nblk * d, nblk * d)


def kernel(x, ffn1_norm, ffn1_wg, ffn1_wu, ffn1_wd, mix_norm, w_in, hgrn_lb_logits, hgrn_norm,
           conv_w, conv_b, lru_wa, lru_ba, lru_wx, lru_bx, lru_lambda, lru_norm,
           sgu_w, sgu_b, sgu_norm, w_out, ffn2_norm, ffn2_wg, ffn2_wu, ffn2_wd, final_norm):
    bsz, seq, d_model = x.shape
    depth = w_in.shape[0]
    d_b = lru_norm.shape[-1]
    d_c = sgu_norm.shape[-1]
    assert seq % MIX_ROWS == 0 and (bsz * seq) % FFN_ROWS == 0
    assert MIX_ROWS % HGRN_CHUNK == 0 and MIX_ROWS % SGU_CHUNK == 0
    assert hgrn_norm.shape[-1] % HGRN_HEAD_DIM == 0

    row = lambda a: a.reshape(depth, 1, a.shape[-1])
    group_id = np.arange(d_b) // (d_b // LRU_BLOCKS)
    g64 = (group_id[:, None] == group_id[None, :]).astype(np.float32) / (d_b // LRU_BLOCKS)
    assert d_c == d_b and SGU_GROUPS == LRU_BLOCKS
    p = {
        "mix_norm": row(mix_norm),
        "w_in": w_in.astype(BF16),
        "lb_logits": hgrn_lb_logits,
        "hgrn_norm": row(hgrn_norm),
        "conv_w": conv_w,
        "conv_b": row(conv_b),
        "wax": jnp.concatenate([_block_diag(lru_wa), _block_diag(lru_wx)], axis=-1).astype(BF16),
        "bax": jnp.concatenate([lru_ba.reshape(depth, 1, d_b), lru_bx.reshape(depth, 1, d_b)], axis=-1),
        "lam": row(lru_lambda),
        "lru_norm": row(lru_norm),
        "sgu_w": sgu_w,
        "sgu_b": jnp.repeat(jnp.swapaxes(sgu_b, 1, 2), d_c // SGU_GROUPS, axis=2),
        "sgu_norm": row(sgu_norm),
        "w_out": w_out.astype(BF16),
        "g64": jnp.asarray(g64, BF16),
    }
    ffn1 = (row(ffn1_norm), ffn1_wg.astype(BF16), ffn1_wu.astype(BF16), ffn1_wd.astype(BF16))
    ffn2 = (row(ffn2_norm), ffn2_wg.astype(BF16), ffn2_wu.astype(BF16), ffn2_wd.astype(BF16))
    fgain = final_norm.reshape(1, d_model)

    h = x.reshape(bsz * seq, d_model)
    for layer in range(depth):
        h = _ffn_call(h, layer, *ffn1, fgain, False)
        h = _mixer_call(h, seq, layer, p)
        h = _ffn_call(h, layer, *ffn2, fgain, layer == depth - 1)
    return h.reshape(bsz, seq, d_model)
```
